```python
import jax, jax.numpy as jnp
from jax import lax
import numpy as np

D_MODEL = 1024
BATCH = 4
SEQ = 4096
DEPTH = 4

N_EVEN = (DEPTH + 1) // 2
N_ODD = DEPTH // 2
EPS = 1e-6
LRU_WIDTH = D_MODEL // 2
LRU_BLOCKS = 8
LRU_BLOCK = LRU_WIDTH // LRU_BLOCKS
CONV_W = 4
LRU_C = 8.0
FOX_HEADS = 8
FOX_HEAD_DIM = (D_MODEL // 2) // FOX_HEADS
FOX_WIDTH = FOX_HEADS * FOX_HEAD_DIM
Q_BLOCK = 128
EVEN_SPLITS = [LRU_WIDTH, 2 * LRU_WIDTH, 2 * LRU_WIDTH + FOX_WIDTH, 2 * LRU_WIDTH + 2 * FOX_WIDTH, 2 * LRU_WIDTH + 3 * FOX_WIDTH]
EVEN_IN = 2 * LRU_WIDTH + 3 * FOX_WIDTH + FOX_HEADS
GLA_HEADS = 4
GLA_DK = (D_MODEL // 2) // GLA_HEADS
GLA_DV = D_MODEL // GLA_HEADS
GLA_RANK = 16
GLA_TAU = 16.0
GLA_CHUNK = 64
GLA_KW = GLA_HEADS * GLA_DK
GLA_VW = GLA_HEADS * GLA_DV
ODD_SPLITS = [GLA_KW, 2 * GLA_KW, 2 * GLA_KW + GLA_VW, 2 * GLA_KW + 2 * GLA_VW]
ODD_IN = 2 * GLA_KW + 2 * GLA_VW + GLA_RANK
D_FF = 2816
N_EXPERTS = 8
TOP_K = 2
D_FF_EXPERT = 3584

kernel_name = "hybrid_rglru_fox_gla_moe"


def rms_norm(x, g):
    xf = x.astype(jnp.float32)
    y = xf * lax.rsqrt(jnp.mean(xf * xf, axis=-1, keepdims=True) + EPS)
    return (y * g.astype(jnp.float32)).astype(x.dtype)


def causal_depthwise_conv(x, w, b):
    s = x.shape[1]
    xp = jnp.pad(x, ((0, 0), (CONV_W - 1, 0), (0, 0)))
    out = b
    for j in range(CONV_W):
        out = out + xp[:, j:j + s] * w[j]
    return out


def block_diag_linear(x, w, b):
    xb = x.reshape(x.shape[0], x.shape[1], LRU_BLOCKS, LRU_BLOCK)
    return jnp.einsum('bsni,nij->bsnj', xb, w).reshape(x.shape) + b


def rglru(x, ga_w, ga_b, gx_w, gx_b, lam):
    xf = x.astype(jnp.float32)
    r = jax.nn.sigmoid(block_diag_linear(xf, ga_w.astype(jnp.float32), ga_b.astype(jnp.float32)))
    i = jax.nn.sigmoid(block_diag_linear(xf, gx_w.astype(jnp.float32), gx_b.astype(jnp.float32)))
    log_a = -LRU_C * r * jax.nn.softplus(-lam.astype(jnp.float32))
    a = jnp.exp(log_a)
    u = jnp.sqrt(-jnp.expm1(2.0 * log_a)) * (i * xf)

    def combine(c1, c2):
        a1, b1 = c1
        a2, b2 = c2
        return a1 * a2, a2 * b1 + b2

    _, h = lax.associative_scan(combine, (a, u), axis=1)
    return h


def fox_attention(q, k, v, log_f):
    bsz, s, h, dh = q.shape
    nb = s // Q_BLOCK
    c = jnp.cumsum(log_f, axis=1).transpose(0, 2, 1)
    qh = q.transpose(0, 2, 1, 3) * (dh ** -0.5)
    kh = k.transpose(0, 2, 1, 3)
    vh = v.transpose(0, 2, 1, 3)
    qb = qh.reshape(bsz, h, nb, Q_BLOCK, dh).transpose(2, 0, 1, 3, 4)
    cb = c.reshape(bsz, h, nb, Q_BLOCK).transpose(2, 0, 1, 3)
    pos_b = jnp.arange(s).reshape(nb, Q_BLOCK)
    k_pos = jnp.arange(s)

    def one_block(args):
        q_blk, c_blk, p_blk = args
        logits = jnp.einsum('bhqd,bhkd->bhqk', q_blk, kh).astype(jnp.float32)
        logits = logits + c_blk[..., None] - c[:, :, None, :]
        mask = p_blk[:, None] >= k_pos[None, :]
        logits = jnp.where(mask, logits, -jnp.inf)
        p = jax.nn.softmax(logits, axis=-1)
        return jnp.einsum('bhqk,bhkd->bhqd', p.astype(vh.dtype), vh)

    ob = lax.map(one_block, (qb, cb, pos_b))
    return ob.transpose(1, 0, 3, 2, 4).reshape(bsz, s, h * dh)


def gla_chunked(q, k, v, log_a):
    bsz, s, h, dk = q.shape
    dv = v.shape[-1]
    n = s // GLA_CHUNK

    def chunks(t):
        return t.reshape(bsz, n, GLA_CHUNK, h, t.shape[-1]).transpose(1, 0, 3, 2, 4)

    qc, kc, vc, gc = chunks(q), chunks(k), chunks(v), chunks(log_a)
    b = jnp.cumsum(gc, axis=3)
    b_last = b[:, :, :, -1:, :]
    q_dec = qc * jnp.exp(b)
    k_intra = kc * jnp.exp(-b)
    k_state = kc * jnp.exp(b_last - b)
    chunk_decay = jnp.exp(b_last[:, :, :, 0, :])
    causal = jnp.tril(jnp.ones((GLA_CHUNK, GLA_CHUNK), dtype=bool))
    scores = jnp.where(causal, jnp.einsum('nbhid,nbhjd->nbhij', q_dec, k_intra), 0.0)
    o_intra = jnp.einsum('nbhij,nbhje->nbhie', scores, vc)

    def step(state, xs):
        q_n, k_n, v_n, d_n = xs
        o_n = jnp.einsum('bhid,bhde->bhie', q_n, state)
        state = d_n[..., None] * state + jnp.einsum('bhjd,bhje->bhde', k_n, v_n)
        return state, o_n

    state0 = jnp.zeros((bsz, h, dk, dv), jnp.float32)
    _, o_inter = lax.scan(step, state0, (q_dec, k_state, vc, chunk_decay))
    o = o_intra + o_inter
    return o.transpose(1, 0, 3, 2, 4).reshape(bsz, s, h, dv)


def even_mixer(hn, w_in, conv_w, conv_b, ga_w, ga_b, gx_w, gx_b, lam, f_b, w_out):
    bsz, s, _ = hn.shape
    proj = hn @ w_in
    xr, yr, q, k, v, f_logit = jnp.split(proj, EVEN_SPLITS, axis=-1)
    xc = causal_depthwise_conv(xr, conv_w, conv_b)
    h_lru = rglru(xc, ga_w, ga_b, gx_w, gx_b, lam)
    y_a = (h_lru * jax.nn.gelu(yr.astype(jnp.float32))).astype(hn.dtype)
    log_f = jax.nn.log_sigmoid((f_logit + f_b).astype(jnp.float32))
    shp = (bsz, s, FOX_HEADS, FOX_HEAD_DIM)
    y_b = fox_attention(q.reshape(shp), k.reshape(shp), v.reshape(shp), log_f).astype(hn.dtype)
    return jnp.concatenate([y_a, y_b], axis=-1) @ w_out


def odd_mixer(hn, w_in, gate_w2, gate_b, head_norm, w_out):
    bsz, s, _ = hn.shape
    proj = hn @ w_in
    q, k, v, g, lr = jnp.split(proj, ODD_SPLITS, axis=-1)
    log_a = jax.nn.log_sigmoid((lr @ gate_w2 + gate_b).astype(jnp.float32)) / GLA_TAU
    kshp = (bsz, s, GLA_HEADS, GLA_DK)
    o = gla_chunked((q.astype(jnp.float32) * (GLA_DK ** -0.5)).reshape(kshp),
                    k.astype(jnp.float32).reshape(kshp),
                    v.astype(jnp.float32).reshape(bsz, s, GLA_HEADS, GLA_DV),
                    log_a.reshape(kshp))
    o = o * lax.rsqrt(jnp.mean(o * o, axis=-1, keepdims=True) + EPS)
    o = o * head_norm.astype(jnp.float32).reshape(GLA_HEADS, GLA_DV)
    o = o.reshape(bsz, s, GLA_VW) * jax.nn.silu(g.astype(jnp.float32))
    return o.astype(hn.dtype) @ w_out


def swiglu(hn, w1, w3, w2):
    return (jax.nn.silu(hn @ w1) * (hn @ w3)) @ w2


def moe_swiglu(hn, router, w1, w3, w2):
    bsz, s, d = hn.shape
    t = hn.reshape(-1, d)
    logits = (t @ router).astype(jnp.float32)
    top_vals, top_idx = lax.top_k(logits, TOP_K)
    top_w = jax.nn.softmax(top_vals, axis=-1)
    gates = jnp.sum(top_w[..., None] * jax.nn.one_hot(top_idx, N_EXPERTS, dtype=jnp.float32), axis=1)
    out = jnp.zeros_like(t)
    for e in range(N_EXPERTS):
        h_e = jax.nn.silu(t @ w1[e]) * (t @ w3[e])
        out = out + gates[:, e:e + 1].astype(t.dtype) * (h_e @ w2[e])
    return out.reshape(bsz, s, d)


def setup_inputs(seed: int = 0) -> dict:
    key = jax.random.key(seed)
    ks = iter(jax.random.split(key, 32))
    f32 = jnp.float32
    out_scale = (2.0 * DEPTH) ** -0.5

    def dense(shape, fan_in, scale=1.0):
        return jax.random.normal(next(ks), shape, f32) * (scale * fan_in ** -0.5)

    def gain(shape):
        return 1.0 + 0.02 * jax.random.normal(next(ks), shape, f32)

    def small(shape):
        return 0.02 * jax.random.normal(next(ks), shape, f32)

    x = jax.random.normal(next(ks), (BATCH, SEQ, D_MODEL), f32)
    norm_mix = gain((DEPTH, D_MODEL))
    norm_ffn = gain((DEPTH, D_MODEL))
    norm_final = gain((D_MODEL,))
    ev_w_in = dense((N_EVEN, D_MODEL, EVEN_IN), D_MODEL)
    ev_conv_w = dense((N_EVEN, CONV_W, LRU_WIDTH), CONV_W)
    ev_conv_b = small((N_EVEN, LRU_WIDTH))
    ev_ga_w = dense((N_EVEN, LRU_BLOCKS, LRU_BLOCK, LRU_BLOCK), LRU_BLOCK)
    ev_ga_b = small((N_EVEN, LRU_WIDTH))
    ev_gx_w = dense((N_EVEN, LRU_BLOCKS, LRU_BLOCK, LRU_BLOCK), LRU_BLOCK)
    ev_gx_b = small((N_EVEN, LRU_WIDTH))
    u = jax.random.uniform(next(ks), (N_EVEN, LRU_WIDTH), f32, minval=0.9, maxval=0.999)
    p = u ** (1.0 / LRU_C)
    ev_lambda = jnp.log(p) - jnp.log1p(-p)
    ev_f_b = 2.0 + 0.1 * jax.random.normal(next(ks), (N_EVEN, FOX_HEADS), f32)
    ev_w_out = dense((N_EVEN, D_MODEL, D_MODEL), D_MODEL, out_scale)
    ev_ffn_w1 = dense((N_EVEN, D_MODEL, D_FF), D_MODEL)
    ev_ffn_w3 = dense((N_EVEN, D_MODEL, D_FF), D_MODEL)
    ev_ffn_w2 = dense((N_EVEN, D_FF, D_MODEL), D_FF, out_scale)
    od_w_in = dense((N_ODD, D_MODEL, ODD_IN), D_MODEL)
    od_gate_w2 = dense((N_ODD, GLA_RANK, GLA_KW), GLA_RANK)
    od_gate_b = small((N_ODD, GLA_KW))
    od_head_norm = gain((N_ODD, GLA_VW))
    od_w_out = dense((N_ODD, GLA_VW, D_MODEL), GLA_VW, out_scale)
    od_router = dense((N_ODD, D_MODEL, N_EXPERTS), D_MODEL)
    od_exp_w1 = dense((N_ODD, N_EXPERTS, D_MODEL, D_FF_EXPERT), D_MODEL)
    od_exp_w3 = dense((N_ODD, N_EXPERTS, D_MODEL, D_FF_EXPERT), D_MODEL)
    od_exp_w2 = dense((N_ODD, N_EXPERTS, D_FF_EXPERT, D_MODEL), D_FF_EXPERT, out_scale)
    return {"x": x, "norm_mix": norm_mix, "norm_ffn": norm_ffn, "norm_final": norm_final,
            "ev_w_in": ev_w_in, "ev_conv_w": ev_conv_w, "ev_conv_b": ev_conv_b,
            "ev_ga_w": ev_ga_w, "ev_ga_b": ev_ga_b, "ev_gx_w": ev_gx_w, "ev_gx_b": ev_gx_b,
            "ev_lambda": ev_lambda, "ev_f_b": ev_f_b, "ev_w_out": ev_w_out,
            "ev_ffn_w1": ev_ffn_w1, "ev_ffn_w3": ev_ffn_w3, "ev_ffn_w2": ev_ffn_w2,
            "od_w_in": od_w_in, "od_gate_w2": od_gate_w2, "od_gate_b": od_gate_b,
            "od_head_norm": od_head_norm, "od_w_out": od_w_out, "od_router": od_router,
            "od_exp_w1": od_exp_w1, "od_exp_w3": od_exp_w3, "od_exp_w2": od_exp_w2}


def reference(x, norm_mix, norm_ffn, norm_final,
              ev_w_in, ev_conv_w, ev_conv_b, ev_ga_w, ev_ga_b, ev_gx_w, ev_gx_b,
              ev_lambda, ev_f_b, ev_w_out, ev_ffn_w1, ev_ffn_w3, ev_ffn_w2,
              od_w_in, od_gate_w2, od_gate_b, od_head_norm, od_w_out, od_router,
              od_exp_w1, od_exp_w3, od_exp_w2):
    h = x
    for layer in range(DEPTH):
        j = layer // 2
        hn = rms_norm(h, norm_mix[layer])
        if layer % 2 == 0:
            h = h + even_mixer(hn, ev_w_in[j], ev_conv_w[j], ev_conv_b[j], ev_ga_w[j], ev_ga_b[j],
                               ev_gx_w[j], ev_gx_b[j], ev_lambda[j], ev_f_b[j], ev_w_out[j])
            h = h + swiglu(rms_norm(h, norm_ffn[layer]), ev_ffn_w1[j], ev_ffn_w3[j], ev_ffn_w2[j])
        else:
            h = h + odd_mixer(hn, od_w_in[j], od_gate_w2[j], od_gate_b[j], od_head_norm[j], od_w_out[j])
            h = h + moe_swiglu(rms_norm(h, norm_ffn[layer]), od_router[j], od_exp_w1[j],
                               od_exp_w3[j], od_exp_w2[j])
    return rms_norm(h, norm_final)
```

```python
import functools

import jax
import jax.numpy as jnp
from jax import lax
from jax.experimental import pallas as pl
from jax.experimental.pallas import tpu as pltpu

F32 = jnp.float32
BF16 = jnp.bfloat16

EPS = 1e-6
D_MODEL = 1024
LRU_WIDTH = 512
LRU_BLOCKS = 8
CONV_W = 4
LRU_C = 8.0
FOX_HEADS = 8
FOX_HEAD_DIM = 64
GLA_HEADS = 4
GLA_DK = 128
GLA_DV = 256
GLA_RANK = 16
GLA_TAU = 16.0
GLA_CHUNK = 64
N_EXPERTS = 8
LANES = 128
SUBLANES = 8
NEG_BIG = -1e30
VMEM_LIMIT = 56 * 1024 * 1024

TM_PROJ = 512
TC_LRU = 256
TQ_ATTN = 512
TM_FFN = 512
FC_FFN = 256
R_GLA = 256
TM_ROUTER = 512
TM_MOE = 512
TM_COMBINE = 512


def _cparams(*sem):
    return pltpu.CompilerParams(dimension_semantics=sem, vmem_limit_bytes=VMEM_LIMIT)


def _rmsnorm(x, g):
    return x * lax.rsqrt(jnp.mean(x * x, axis=-1, keepdims=True) + EPS) * g


def _sigmoid(x):
    return 1.0 / (1.0 + jnp.exp(-x))


def _log_sigmoid(z):
    return jnp.minimum(z, 0.0) - jnp.log1p(jnp.exp(-jnp.abs(z)))


def _dot(a, b):
    return jnp.dot(a, b, preferred_element_type=F32)


def _dot_nt(a, b):
    return lax.dot_general(a, b, (((1,), (1,)), ((), ())), preferred_element_type=F32)


def _dot_tn(a, b):
    return lax.dot_general(a, b, (((0,), (0,)), ((), ())), preferred_element_type=F32)


def _proj_kernel(h_ref, g_ref, w_ref, *out_refs, splits):
    xn = _rmsnorm(h_ref[...], g_ref[...]).astype(BF16)
    for o_ref, (start, width) in zip(out_refs, splits):
        o_ref[...] = _dot(xn, w_ref[:, start:start + width]).astype(o_ref.dtype)


def _proj(h, g, w, outs):
    t, d = h.shape
    n = w.shape[1]
    tm = TM_PROJ
    splits = tuple((s, wd) for s, wd, _ in outs)
    return pl.pallas_call(
        functools.partial(_proj_kernel, splits=splits),
        grid=(t // tm,),
        in_specs=[pl.BlockSpec((tm, d), lambda i: (i, 0)),
                  pl.BlockSpec((1, d), lambda i: (0, 0)),
                  pl.BlockSpec((d, n), lambda i: (0, 0))],
        out_specs=[pl.BlockSpec((tm, wd), lambda i: (i, 0)) for _, wd, _ in outs],
        out_shape=[jax.ShapeDtypeStruct((t, wd), dt) for _, wd, dt in outs],
        compiler_params=_cparams("parallel"),
        name="proj",
    )(h, g, w)


def _scan8(a, u):
    row = lax.broadcasted_iota(jnp.int32, u.shape, 0) & (SUBLANES - 1)
    for d in (1, 2, 4):
        m = row >= d
        u_sh = pltpu.roll(u, d, 0)
        if a is None:
            u = jnp.where(m, u + u_sh, u)
        else:
            a_sh = pltpu.roll(a, d, 0)
            u = jnp.where(m, a * u_sh + u, u)
            a = jnp.where(m, a * a_sh, a)
    return a, u


def _lru_kernel(xr_ref, yr_ref, f_ref, cw_ref, cb_ref, wga_ref, wgx_ref, gab_ref, gxb_ref,
                lam_ref, fb_ref, ya_ref, c_ref, xprev, hprev, cprev, hs, *, tc):
    @pl.when(pl.program_id(1) == 0)
    def _():
        xprev[...] = jnp.zeros_like(xprev)
        hprev[...] = jnp.zeros_like(hprev)
        cprev[...] = jnp.zeros_like(cprev)

    x = xr_ref[...]
    p8 = xprev[...]
    row8 = lax.broadcasted_iota(jnp.int32, p8.shape, 0)
    xc = cb_ref[...] + cw_ref[CONV_W - 1:CONV_W, :] * x
    for j in range(CONV_W - 1):
        sh = CONV_W - 1 - j
        xs = pltpu.roll(x, sh, 0)
        head = jnp.where(row8 < sh, pltpu.roll(p8, sh, 0), xs[:SUBLANES])
        xs = jnp.concatenate([head, xs[SUBLANES:]], axis=0)
        xc = xc + cw_ref[j:j + 1, :] * xs
    xprev[...] = x[tc - SUBLANES:, :]

    xb = xc.astype(BF16)
    r = _sigmoid(_dot(xb, wga_ref[...]) + gab_ref[...])
    i = _sigmoid(_dot(xb, wgx_ref[...]) + gxb_ref[...])
    nl = -lam_ref[...]
    softplus_nl = jnp.maximum(nl, 0.0) + jnp.log1p(jnp.exp(-jnp.abs(nl)))
    log_a = (-LRU_C) * r * softplus_nl
    a = jnp.exp(log_a)
    u = jnp.sqrt(-jnp.tanh(log_a) * (1.0 + a * a)) * (i * xc)

    a8, u8 = _scan8(a, u)
    _, lf8 = _scan8(None, _log_sigmoid(f_ref[...] + fb_ref[...]))

    h = hprev[...]
    c = cprev[...]
    for gi in range(tc // SUBLANES):
        rows = slice(gi * SUBLANES, (gi + 1) * SUBLANES)
        blk = a8[rows] * h + u8[rows]
        hs[rows, :] = blk
        h = blk[SUBLANES - 1:SUBLANES, :]
        cblk = lf8[rows] + c
        c_ref[rows, :] = cblk
        c = cblk[SUBLANES - 1:SUBLANES, :]
    hprev[...] = h
    cprev[...] = c

    yr = yr_ref[...]
    gelu = 0.5 * yr * (1.0 + jnp.tanh(0.7978845608028654 * (yr + 0.044715 * (yr * yr * yr))))
    ya_ref[...] = (hs[...] * gelu).astype(BF16)


def _lru(xy, f, cw, cb, wga, wgx, gab, gxb, lam, fb, bsz, seq):
    t = xy.shape[0]
    w = LRU_WIDTH
    tc = TC_LRU
    ns = seq // tc
    row = lambda b, s: (b * ns + s, 0)
    const = lambda b, s: (0, 0)
    return pl.pallas_call(
        functools.partial(_lru_kernel, tc=tc),
        grid=(bsz, ns),
        in_specs=[pl.BlockSpec((tc, w), row),
                  pl.BlockSpec((tc, w), lambda b, s: (b * ns + s, 1)),
                  pl.BlockSpec((tc, LANES), row),
                  pl.BlockSpec((CONV_W, w), const),
                  pl.BlockSpec((1, w), const),
                  pl.BlockSpec((w, w), const),
                  pl.BlockSpec((w, w), const),
                  pl.BlockSpec((1, w), const),
                  pl.BlockSpec((1, w), const),
                  pl.BlockSpec((1, w), const),
                  pl.BlockSpec((1, LANES), const)],
        out_specs=[pl.BlockSpec((tc, w), row), pl.BlockSpec((tc, LANES), row)],
        out_shape=[jax.ShapeDtypeStruct((t, w), BF16), jax.ShapeDtypeStruct((t, LANES), F32)],
        scratch_shapes=[pltpu.VMEM((SUBLANES, w), F32), pltpu.VMEM((1, w), F32),
                        pltpu.VMEM((1, LANES), F32), pltpu.VMEM((tc, w), F32)],
        compiler_params=_cparams("parallel", "arbitrary"),
        name="lru",
    )(xy, xy, f, cw, cb, wga, wgx, gab, gxb, lam, fb)


def _attn_kernel(q_ref, k_ref, v_ref, cq_ref, ck_ref, o_ref, m_sc, l_sc, acc_sc, cq_sc, *, tq):
    hp = pl.program_id(1)
    qi = pl.program_id(2)
    ki = pl.program_id(3)
    lane = lax.broadcasted_iota(jnp.int32, (1, LANES), 1)

    @pl.when(ki == 0)
    def _():
        m_sc[...] = jnp.full_like(m_sc, NEG_BIG)
        l_sc[...] = jnp.zeros_like(l_sc)
        acc_sc[...] = jnp.zeros_like(acc_sc)
        cb = cq_ref[...]
        for hh in range(2):
            cq_sc[hh] = jnp.sum(jnp.where(lane == 2 * hp + hh, cb, 0.0), axis=-1, keepdims=True)

    def step(masked):
        q2 = q_ref[...]
        k2 = k_ref[...]
        v2 = v_ref[...]
        for hh in range(2):
            in_head = (lane >= FOX_HEAD_DIM * hh) & (lane < FOX_HEAD_DIM * (hh + 1))
            qm = jnp.where(in_head, q2, jnp.zeros_like(q2)) * jnp.asarray(FOX_HEAD_DIM ** -0.5, BF16)
            s = _dot_nt(qm, k2) + cq_sc[hh] - ck_ref[hh:hh + 1, :]
            if masked:
                rowi = lax.broadcasted_iota(jnp.int32, s.shape, 0)
                coli = lax.broadcasted_iota(jnp.int32, s.shape, 1)
                s = jnp.where(rowi >= coli, s, NEG_BIG)
            m_prev = m_sc[hh]
            m_new = jnp.maximum(m_prev, jnp.max(s, axis=-1, keepdims=True))
            alpha = jnp.exp(m_prev - m_new)
            p = jnp.exp(s - m_new)
            l_sc[hh] = alpha * l_sc[hh] + jnp.sum(p, axis=-1, keepdims=True)
            acc_sc[hh] = alpha * acc_sc[hh] + _dot(p.astype(BF16), v2)
            m_sc[hh] = m_new

    @pl.when(ki < qi)
    def _():
        step(False)

    @pl.when(ki == qi)
    def _():
        step(True)
        o = jnp.where(lane < FOX_HEAD_DIM, acc_sc[0] / l_sc[0], acc_sc[1] / l_sc[1])
        o_ref[...] = o.astype(o_ref.dtype)


def _attn(qkv, c, ct, bsz, seq):
    t = qkv.shape[0]
    tq = min(TQ_ATTN, seq)
    nq = seq // tq
    npair = FOX_HEADS // 2
    return pl.pallas_call(
        functools.partial(_attn_kernel, tq=tq),
        grid=(bsz, npair, nq, nq),
        in_specs=[pl.BlockSpec((tq, LANES), lambda b, h, qi, ki: (b * nq + qi, h)),
                  pl.BlockSpec((tq, LANES), lambda b, h, qi, ki: (b * nq + jnp.minimum(ki, qi), npair + h)),
                  pl.BlockSpec((tq, LANES), lambda b, h, qi, ki: (b * nq + jnp.minimum(ki, qi), 2 * npair + h)),
                  pl.BlockSpec((tq, LANES), lambda b, h, qi, ki: (b * nq + qi, 0)),
                  pl.BlockSpec((None, None, 2, tq), lambda b, h, qi, ki: (b, h, 0, jnp.minimum(ki, qi)))],
        out_specs=pl.BlockSpec((tq, LANES), lambda b, h, qi, ki: (b * nq + qi, h)),
        out_shape=jax.ShapeDtypeStruct((t, FOX_HEADS * FOX_HEAD_DIM), BF16),
        scratch_shapes=[pltpu.VMEM((2, tq, 1), F32), pltpu.VMEM((2, tq, 1), F32),
                        pltpu.VMEM((2, tq, LANES), F32), pltpu.VMEM((2, tq, 1), F32)],
        compiler_params=_cparams("parallel", "parallel", "parallel", "arbitrary"),
        name="attn",
    )(qkv, qkv, qkv, c, ct)


def _swiglu(xn, w1_ref, w3_ref, w2_ref):
    nc = w1_ref.shape[0]

    def body(c, acc):
        a = _dot(xn, w1_ref[c])
        b = _dot(xn, w3_ref[c])
        hc = (a * _sigmoid(a) * b).astype(BF16)
        return acc + _dot(hc, w2_ref[c])

    return lax.fori_loop(0, nc, body, jnp.zeros((xn.shape[0], w2_ref.shape[2]), F32))


def _ffn_kernel(h_ref, ya_ref, yb_ref, wo_ref, g_ref, w1_ref, w3_ref, w2_ref, o_ref):
    half = ya_ref.shape[1]
    h1 = h_ref[...] + _dot(ya_ref[...], wo_ref[:half, :]) + _dot(yb_ref[...], wo_ref[half:, :])
    xn = _rmsnorm(h1, g_ref[...]).astype(BF16)
    o_ref[...] = h1 + _swiglu(xn, w1_ref, w3_ref, w2_ref)


def _ffn(h, ya, yb, wo, g, w1, w3, w2):
    t, d = h.shape
    tm = TM_FFN
    nc, _, fc = w1.shape
    half = ya.shape[1]
    row = lambda i: (i, 0)
    c2 = lambda i: (0, 0)
    c3 = lambda i: (0, 0, 0)
    return pl.pallas_call(
        _ffn_kernel,
        grid=(t // tm,),
        in_specs=[pl.BlockSpec((tm, d), row), pl.BlockSpec((tm, half), row), pl.BlockSpec((tm, half), row),
                  pl.BlockSpec((d, d), c2), pl.BlockSpec((1, d), c2),
                  pl.BlockSpec((nc, d, fc), c3), pl.BlockSpec((nc, d, fc), c3), pl.BlockSpec((nc, fc, d), c3)],
        out_specs=pl.BlockSpec((tm, d), row),
        out_shape=jax.ShapeDtypeStruct((t, d), F32),
        compiler_params=_cparams("parallel"),
        name="ffn",
    )(h, ya, yb, wo, g, w1, w3, w2)


def _gla_kernel(q_ref, k_ref, v_ref, g_ref, lr_ref, gw_ref, gb_ref, hn_ref, o_ref, state_t, *, rows):
    @pl.when(pl.program_id(2) == 0)
    def _():
        state_t[...] = jnp.zeros_like(state_t)

    cs = GLA_CHUNK
    log_a = _log_sigmoid(_dot(lr_ref[...].astype(BF16), gw_ref[...]) + gb_ref[...]) * (1.0 / GLA_TAU)
    rowi = lax.broadcasted_iota(jnp.int32, (cs, GLA_DK), 0)
    tri = (lax.broadcasted_iota(jnp.int32, (cs, cs), 0) >= lax.broadcasted_iota(jnp.int32, (cs, cs), 1))
    for ci in range(rows // cs):
        sl = slice(ci * cs, (ci + 1) * cs)
        b = log_a[sl]
        d = 1
        while d < cs:
            b = b + jnp.where(rowi >= d, pltpu.roll(b, d, 0), 0.0)
            d *= 2
        b_last = b[cs - 1:cs, :]
        q = q_ref[sl, :] * (GLA_DK ** -0.5)
        k = k_ref[sl, :]
        v = v_ref[sl, :]
        q_dec = (q * jnp.exp(b)).astype(BF16)
        k_intra = (k * jnp.exp(-b)).astype(BF16)
        k_state = (k * jnp.exp(b_last - b)).astype(BF16)
        scores = jnp.where(tri, _dot_nt(q_dec, k_intra), 0.0)
        st = state_t[...]
        o = _dot(scores.astype(BF16), v) + _dot_nt(q_dec, st.astype(BF16))
        state_t[...] = st * jnp.exp(b_last) + _dot_tn(v, k_state)
        o = o * lax.rsqrt(jnp.mean(o * o, axis=-1, keepdims=True) + EPS) * hn_ref[...]
        g = g_ref[sl, :]
        o_ref[sl, :] = (o * (g * _sigmoid(g))).astype(o_ref.dtype)


def _gla(qk, v, g, lr, gw, gb, hn, bsz, seq):
    t = qk.shape[0]
    rows = R_GLA
    nr = seq // rows
    nh = GLA_HEADS
    return pl.pallas_call(
        functools.partial(_gla_kernel, rows=rows),
        grid=(bsz, nh, nr),
        in_specs=[pl.BlockSpec((rows, GLA_DK), lambda b, h, r: (b * nr + r, h)),
                  pl.BlockSpec((rows, GLA_DK), lambda b, h, r: (b * nr + r, nh + h)),
                  pl.BlockSpec((rows, GLA_DV), lambda b, h, r: (b * nr + r, h)),
                  pl.BlockSpec((rows, GLA_DV), lambda b, h, r: (b * nr + r, h)),
                  pl.BlockSpec((rows, LANES), lambda b, h, r: (b * nr + r, 0)),
                  pl.BlockSpec((LANES, GLA_DK), lambda b, h, r: (0, h)),
                  pl.BlockSpec((1, GLA_DK), lambda b, h, r: (0, h)),
                  pl.BlockSpec((1, GLA_DV), lambda b, h, r: (0, h))],
        out_specs=pl.BlockSpec((rows, GLA_DV), lambda b, h, r: (b * nr + r, h)),
        out_shape=jax.ShapeDtypeStruct((t, nh * GLA_DV), BF16),
        scratch_shapes=[pltpu.VMEM((GLA_DV, GLA_DK), F32)],
        compiler_params=_cparams("parallel", "parallel", "arbitrary"),
        name="gla",
    )(qk, qk, v, g, lr, gw, gb, hn)


def _router_kernel(h_ref, o_ref, wo_ref, g_ref, rw_ref, h1_ref, xn_ref, info_ref):
    h1 = h_ref[...] + _dot(o_ref[...], wo_ref[...])
    h1_ref[...] = h1
    xn = _rmsnorm(h1, g_ref[...])
    xn_ref[...] = xn.astype(xn_ref.dtype)
    logits = jnp.dot(xn, rw_ref[...], preferred_element_type=F32, precision=lax.Precision.HIGHEST)
    lane = lax.broadcasted_iota(jnp.int32, logits.shape, 1)
    lg = jnp.where(lane < N_EXPERTS, logits, -jnp.inf)
    m1 = jnp.max(lg, axis=-1, keepdims=True)
    i1 = jnp.min(jnp.where(lg == m1, lane, LANES), axis=-1, keepdims=True)
    lg2 = jnp.where(lane == i1, -jnp.inf, lg)
    m2 = jnp.max(lg2, axis=-1, keepdims=True)
    i2 = jnp.min(jnp.where(lg2 == m2, lane, LANES), axis=-1, keepdims=True)
    e = jnp.exp(m2 - m1)
    w1 = 1.0 / (1.0 + e)
    w2 = e * w1
    info = jnp.where(lane == 0, i1.astype(F32),
                     jnp.where(lane == 1, i2.astype(F32),
                               jnp.where(lane == 2, w1, jnp.where(lane == 3, w2, 0.0))))
    info_ref[...] = info


def _router(h, o, wo, g, rw):
    t, d = h.shape
    tm = TM_ROUTER
    row = lambda i: (i, 0)
    c2 = lambda i: (0, 0)
    return pl.pallas_call(
        _router_kernel,
        grid=(t // tm,),
        in_specs=[pl.BlockSpec((tm, d), row), pl.BlockSpec((tm, d), row), pl.BlockSpec((d, d), c2),
                  pl.BlockSpec((1, d), c2), pl.BlockSpec((d, LANES), c2)],
        out_specs=[pl.BlockSpec((tm, d), row), pl.BlockSpec((tm, d), row), pl.BlockSpec((tm, LANES), row)],
        out_shape=[jax.ShapeDtypeStruct((t, d), F32), jax.ShapeDtypeStruct((t, d), BF16),
                   jax.ShapeDtypeStruct((t, LANES), F32)],
        compiler_params=_cparams("parallel"),
        name="router",
    )(h, o, wo, g, rw)


def _moe_kernel(te_ref, nu_ref, xs_ref, w1_ref, w3_ref, w2_ref, y_ref):
    i = pl.program_id(0)

    @pl.when(i < nu_ref[0])
    def _():
        y_ref[...] = _swiglu(xs_ref[...], w1_ref, w3_ref, w2_ref).astype(y_ref.dtype)

    @pl.when(i >= nu_ref[0])
    def _():
        y_ref[...] = jnp.zeros_like(y_ref)


def _moe(tile_expert, n_used, xs, w1, w3, w2):
    p, d = xs.shape
    tm = TM_MOE
    _, nc, _, fc = w1.shape
    grid_spec = pltpu.PrefetchScalarGridSpec(
        num_scalar_prefetch=2,
        grid=(p // tm,),
        in_specs=[pl.BlockSpec((tm, d), lambda i, te, nu: (i, 0)),
                  pl.BlockSpec((None, nc, d, fc), lambda i, te, nu: (te[i], 0, 0, 0)),
                  pl.BlockSpec((None, nc, d, fc), lambda i, te, nu: (te[i], 0, 0, 0)),
                  pl.BlockSpec((None, nc, fc, d), lambda i, te, nu: (te[i], 0, 0, 0))],
        out_specs=pl.BlockSpec((tm, d), lambda i, te, nu: (i, 0)),
    )
    return pl.pallas_call(
        _moe_kernel,
        grid_spec=grid_spec,
        out_shape=jax.ShapeDtypeStruct((p, d), F32),
        compiler_params=_cparams("arbitrary"),
        name="moe",
    )(tile_expert, n_used, xs, w1, w3, w2)


def _combine_kernel(h1_ref, y1_ref, y2_ref, info_ref, gf_ref, o_ref, *, final):
    info = info_ref[...]
    out = h1_ref[...] + info[:, 2:3] * y1_ref[...] + info[:, 3:4] * y2_ref[...]
    if final:
        out = _rmsnorm(out, gf_ref[...])
    o_ref[...] = out


def _combine(h1, yg, info, gf, final):
    t, d = h1.shape
    tm = TM_COMBINE
    nt = t // tm
    return pl.pallas_call(
        functools.partial(_combine_kernel, final=final),
        grid=(nt,),
        in_specs=[pl.BlockSpec((tm, d), lambda i: (i, 0)),
                  pl.BlockSpec((tm, d), lambda i: (i, 0)),
                  pl.BlockSpec((tm, d), lambda i: (nt + i, 0)),
                  pl.BlockSpec((tm, LANES), lambda i: (i, 0)),
                  pl.BlockSpec((1, d), lambda i: (0, 0))],
        out_specs=pl.BlockSpec((tm, d), lambda i: (i, 0)),
        out_shape=jax.ShapeDtypeStruct((t, d), F32),
        compiler_params=_cparams("parallel"),
        name="combine",
    )(h1, yg, yg, info, gf)


def _routing_plan(info, t):
    tm = TM_MOE
    n_tiles = (2 * t) // tm + N_EXPERTS
    top = jnp.concatenate([info[:, 0], info[:, 1]]).astype(jnp.int32)
    onehot = (top[:, None] == jnp.arange(N_EXPERTS, dtype=jnp.int32)[None, :]).astype(jnp.int32)
    csum = jnp.cumsum(onehot, axis=0)
    rank = jnp.sum((csum - onehot) * onehot, axis=1)
    counts = csum[-1]
    ntile = (counts + tm - 1) // tm
    tile_end = jnp.cumsum(ntile)
    tile_start = tile_end - ntile
    pos = tile_start[top] * tm + rank
    n_used = tile_end[-1]
    tile_ids = jnp.arange(n_tiles, dtype=jnp.int32)
    te = jnp.sum((tile_ids[:, None] >= tile_end[None, :]).astype(jnp.int32), axis=1)
    last = jnp.sum((n_used - 1 >= tile_end).astype(jnp.int32))
    te = jnp.where(tile_ids < n_used, te, last).astype(jnp.int32)
    tok = jnp.arange(t, dtype=jnp.int32)
    token_of = jnp.zeros((n_tiles * tm,), jnp.int32).at[pos].set(jnp.concatenate([tok, tok]))
    return te, n_used.reshape(1).astype(jnp.int32), token_of, pos


def _pad_cols(w, n):
    return jnp.pad(w, ((0, 0), (0, n - w.shape[1])))


def _block_diag(w):
    nb, bs, _ = w.shape
    eye = jnp.eye(nb, dtype=w.dtype)
    return jnp.einsum('nij,nm->nimj', w, eye).reshape(nb * bs, nb * bs)


def _chunk_in(w, fc):
    *lead, d, f = w.shape
    w = w.reshape(*lead, d, f // fc, fc)
    return jnp.swapaxes(w, -3, -2)


def _chunk_out(w, fc):
    *lead, f, d = w.shape
    return w.reshape(*lead, f // fc, fc, d)


def kernel(x, norm_mix, norm_ffn, norm_final, ev_w_in, ev_conv_w, ev_conv_b, ev_ga_w, ev_ga_b, ev_gx_w, ev_gx_b, ev_lambda, ev_f_b, ev_w_out, ev_ffn_w1, ev_ffn_w3, ev_ffn_w2, od_w_in, od_gate_w2, od_gate_b, od_head_norm, od_w_out, od_router, od_exp_w1, od_exp_w3, od_exp_w2):
    bsz, seq, d = x.shape
    t = bsz * seq
    depth = norm_mix.shape[0]
    h = x.reshape(t, d)
    n_even_main = 2 * LRU_WIDTH + 3 * FOX_HEADS * FOX_HEAD_DIM
    n_odd_main = 2 * GLA_HEADS * GLA_DK + 2 * GLA_HEADS * GLA_DV
    row = lambda v: v.reshape(1, -1)

    for layer in range(depth):
        j = layer // 2
        g_mix = row(norm_mix[layer])
        g_ffn = row(norm_ffn[layer])
        if layer % 2 == 0:
            w_in = ev_w_in[j]
            w = jnp.concatenate([w_in[:, :n_even_main], _pad_cols(w_in[:, n_even_main:], LANES)], axis=1).astype(BF16)
            xy, qkv, f = _proj(h, g_mix, w, [(0, 2 * LRU_WIDTH, F32), (2 * LRU_WIDTH, 1536, BF16), (n_even_main, LANES, F32)])
            fb = _pad_cols(row(ev_f_b[j]), LANES)
            ya, c = _lru(xy, f, ev_conv_w[j], row(ev_conv_b[j]),
                         _block_diag(ev_ga_w[j]).astype(BF16), _block_diag(ev_gx_w[j]).astype(BF16),
                         row(ev_ga_b[j]), row(ev_gx_b[j]), row(ev_lambda[j]), fb, bsz, seq)
            ct = c[:, :FOX_HEADS].reshape(bsz, seq, FOX_HEADS // 2, 2).transpose(0, 2, 3, 1)
            yb = _attn(qkv, c, ct, bsz, seq)
            h = _ffn(h, ya, yb, ev_w_out[j].astype(BF16), g_ffn,
                     _chunk_in(ev_ffn_w1[j].astype(BF16), FC_FFN), _chunk_in(ev_ffn_w3[j].astype(BF16), FC_FFN),
                     _chunk_out(ev_ffn_w2[j].astype(BF16), FC_FFN))
        else:
            w_in = od_w_in[j]
            w = jnp.concatenate([w_in[:, :n_odd_main], _pad_cols(w_in[:, n_odd_main:], LANES)], axis=1).astype(BF16)
            qk, v, g, lr = _proj(h, g_mix, w, [(0, 1024, F32), (1024, 1024, BF16), (2048, 1024, F32), (n_odd_main, LANES, F32)])
            gw = jnp.pad(od_gate_w2[j], ((0, LANES - GLA_RANK), (0, 0))).astype(BF16)
            o = _gla(qk, v, g, lr, gw, row(od_gate_b[j]), row(od_head_norm[j]), bsz, seq)
            rw = _pad_cols(od_router[j], LANES)
            h1, xn, info = _router(h, o, od_w_out[j].astype(BF16), g_ffn, rw)
            te, n_used, token_of, pos = _routing_plan(info, t)
            xs = jnp.take(xn, token_of, axis=0)
            y = _moe(te, n_used, xs,
                     _chunk_in(od_exp_w1[j].astype(BF16), FC_FFN), _chunk_in(od_exp_w3[j].astype(BF16), FC_FFN),
                     _chunk_out(od_exp_w2[j].astype(BF16), FC_FFN))
            yg = jnp.take(y, pos, axis=0)
            final = layer == depth - 1
            h = _combine(h1, yg, info, row(norm_final), final)
    if depth % 2 == 1:
        raise NotImplementedError("final norm is fused into the last odd layer")
    return h.reshape(bsz, seq, d)
```

```python
import functools

import jax
import jax.numpy as jnp
import numpy as np
from jax import lax
from jax.experimental import pallas as pl
from jax.experimental.pallas import tpu as pltpu

F32 = jnp.float32
BF16 = jnp.bfloat16

EPS = 1e-6
D_MODEL = 1024
LRU_WIDTH = 512
LRU_BLOCKS = 8
CONV_W = 4
LRU_C = 8.0
FOX_HEADS = 8
FOX_HEAD_DIM = 64
GLA_HEADS = 4
GLA_DK = 128
GLA_DV = 256
GLA_RANK = 16
GLA_TAU = 16.0
GLA_CHUNK = 64
N_EXPERTS = 8
LANES = 128
SUBLANES = 8
NEG_BIG = -1e30
VMEM_LIMIT = 56 * 1024 * 1024
VMEM_LIMIT_MOE = 60 * 1024 * 1024

TM_PROJ = 512
TC_LRU = 256
TQ_ATTN = 512
TM_FFN = 512
FC_FFN = 256
R_GLA = 256
TM_ROUTER = 512
TM_MOE = 256
FC_MOE = 512
TM_COMBINE = 512


def _cparams(*sem):
    return pltpu.CompilerParams(dimension_semantics=sem, vmem_limit_bytes=VMEM_LIMIT)


def _rmsnorm(x, g):
    return x * lax.rsqrt(jnp.mean(x * x, axis=-1, keepdims=True) + EPS) * g


def _sigmoid(x):
    return 1.0 / (1.0 + jnp.exp(-x))


def _log_sigmoid(z):
    return jnp.minimum(z, 0.0) - jnp.log1p(jnp.exp(-jnp.abs(z)))


def _dot(a, b):
    return jnp.dot(a, b, preferred_element_type=F32)


def _dot_nt(a, b):
    return lax.dot_general(a, b, (((1,), (1,)), ((), ())), preferred_element_type=F32)


def _dot_tn(a, b):
    return lax.dot_general(a, b, (((0,), (0,)), ((), ())), preferred_element_type=F32)


def _proj_kernel(h_ref, g_ref, w_ref, *out_refs, splits):
    xn = _rmsnorm(h_ref[...], g_ref[...]).astype(BF16)
    for o_ref, (start, width) in zip(out_refs, splits):
        o_ref[...] = _dot(xn, w_ref[:, start:start + width]).astype(o_ref.dtype)


def _proj(h, g, w, outs):
    t, d = h.shape
    n = w.shape[1]
    tm = TM_PROJ
    splits = tuple((s, wd) for s, wd, _ in outs)
    return pl.pallas_call(
        functools.partial(_proj_kernel, splits=splits),
        grid=(t // tm,),
        in_specs=[pl.BlockSpec((tm, d), lambda i: (i, 0)),
                  pl.BlockSpec((1, d), lambda i: (0, 0)),
                  pl.BlockSpec((d, n), lambda i: (0, 0))],
        out_specs=[pl.BlockSpec((tm, wd), lambda i: (i, 0)) for _, wd, _ in outs],
        out_shape=[jax.ShapeDtypeStruct((t, wd), dt) for _, wd, dt in outs],
        compiler_params=_cparams("parallel"),
        name="proj",
    )(h, g, w)


def _scan8(a, u):
    row = lax.broadcasted_iota(jnp.int32, u.shape, 0) & (SUBLANES - 1)
    for d in (1, 2, 4):
        m = row >= d
        u_sh = pltpu.roll(u, d, 0)
        if a is None:
            u = jnp.where(m, u + u_sh, u)
        else:
            a_sh = pltpu.roll(a, d, 0)
            u = jnp.where(m, a * u_sh + u, u)
            a = jnp.where(m, a * a_sh, a)
    return a, u


def _lru_kernel(xr_ref, yr_ref, f_ref, cw_ref, cb_ref, wga_ref, wgx_ref, gab_ref, gxb_ref,
                lam_ref, fb_ref, ya_ref, c_ref, xprev, hprev, cprev, hs, *, tc):
    @pl.when(pl.program_id(1) == 0)
    def _():
        xprev[...] = jnp.zeros_like(xprev)
        hprev[...] = jnp.zeros_like(hprev)
        cprev[...] = jnp.zeros_like(cprev)

    x = xr_ref[...]
    p8 = xprev[...]
    row8 = lax.broadcasted_iota(jnp.int32, p8.shape, 0)
    xc = cb_ref[...] + cw_ref[CONV_W - 1:CONV_W, :] * x
    for j in range(CONV_W - 1):
        sh = CONV_W - 1 - j
        xs = pltpu.roll(x, sh, 0)
        head = jnp.where(row8 < sh, pltpu.roll(p8, sh, 0), xs[:SUBLANES])
        xs = jnp.concatenate([head, xs[SUBLANES:]], axis=0)
        xc = xc + cw_ref[j:j + 1, :] * xs
    xprev[...] = x[tc - SUBLANES:, :]

    xb = xc.astype(BF16)
    r = _sigmoid(_dot(xb, wga_ref[...]) + gab_ref[...])
    i = _sigmoid(_dot(xb, wgx_ref[...]) + gxb_ref[...])
    nl = -lam_ref[...]
    softplus_nl = jnp.maximum(nl, 0.0) + jnp.log1p(jnp.exp(-jnp.abs(nl)))
    log_a = (-LRU_C) * r * softplus_nl
    a = jnp.exp(log_a)
    u = jnp.sqrt(-jnp.tanh(log_a) * (1.0 + a * a)) * (i * xc)

    a8, u8 = _scan8(a, u)
    _, lf8 = _scan8(None, _log_sigmoid(f_ref[...] + fb_ref[...]))

    h = hprev[...]
    c = cprev[...]
    for gi in range(tc // SUBLANES):
        rows = slice(gi * SUBLANES, (gi + 1) * SUBLANES)
        blk = a8[rows] * h + u8[rows]
        hs[rows, :] = blk
        h = blk[SUBLANES - 1:SUBLANES, :]
        cblk = lf8[rows] + c
        c_ref[rows, :] = cblk
        c = cblk[SUBLANES - 1:SUBLANES, :]
    hprev[...] = h
    cprev[...] = c

    yr = yr_ref[...]
    gelu = 0.5 * yr * (1.0 + jnp.tanh(0.7978845608028654 * (yr + 0.044715 * (yr * yr * yr))))
    ya_ref[...] = (hs[...] * gelu).astype(BF16)


def _lru(xy, f, cw, cb, wga, wgx, gab, gxb, lam, fb, bsz, seq):
    t = xy.shape[0]
    w = LRU_WIDTH
    tc = TC_LRU
    ns = seq // tc
    row = lambda b, s: (b * ns + s, 0)
    const = lambda b, s: (0, 0)
    return pl.pallas_call(
        functools.partial(_lru_kernel, tc=tc),
        grid=(bsz, ns),
        in_specs=[pl.BlockSpec((tc, w), row),
                  pl.BlockSpec((tc, w), lambda b, s: (b * ns + s, 1)),
                  pl.BlockSpec((tc, LANES), row),
                  pl.BlockSpec((CONV_W, w), const),
                  pl.BlockSpec((1, w), const),
                  pl.BlockSpec((w, w), const),
                  pl.BlockSpec((w, w), const),
                  pl.BlockSpec((1, w), const),
                  pl.BlockSpec((1, w), const),
                  pl.BlockSpec((1, w), const),
                  pl.BlockSpec((1, LANES), const)],
        out_specs=[pl.BlockSpec((tc, w), row), pl.BlockSpec((tc, LANES), row)],
        out_shape=[jax.ShapeDtypeStruct((t, w), BF16), jax.ShapeDtypeStruct((t, LANES), F32)],
        scratch_shapes=[pltpu.VMEM((SUBLANES, w), F32), pltpu.VMEM((1, w), F32),
                        pltpu.VMEM((1, LANES), F32), pltpu.VMEM((tc, w), F32)],
        compiler_params=_cparams("parallel", "arbitrary"),
        name="lru",
    )(xy, xy, f, cw, cb, wga, wgx, gab, gxb, lam, fb)


def _placement_matrices():
    nh, hd = FOX_HEADS, FOX_HEAD_DIM
    w = nh * hd
    pq = np.zeros((nh * LANES, w + 3 * LANES), np.float32)
    pk = np.zeros((w + 3 * LANES, nh * LANES), np.float32)
    for h in range(nh):
        for j in range(hd):
            pq[h * LANES + j, h * hd + j] = hd ** -0.5
            pk[h * hd + j, h * LANES + j] = 1.0
        for part in range(3):
            pq[h * LANES + hd + part, w + part * LANES + h] = 1.0
            pk[w + part * LANES + h, h * LANES + hd + 3 + part] = -1.0
    return jnp.asarray(pq, BF16), jnp.asarray(pk, BF16), jnp.eye(w, dtype=BF16)


def _attn_prep_kernel(qkv_ref, c_ref, pq_ref, pk_ref, eye_ref, qt_ref, k_ref, vt_ref):
    hd = FOX_HEAD_DIM
    w = FOX_HEADS * hd
    c = c_ref[...]
    hi = c.astype(BF16)
    r1 = c - hi.astype(F32)
    mid = r1.astype(BF16)
    lo = (r1 - mid.astype(F32)).astype(BF16)
    cs = jnp.concatenate([hi, mid, lo], axis=1)
    qkv = qkv_ref[...]
    xq = jnp.concatenate([qkv[:, :w], cs], axis=1)
    xk = jnp.concatenate([qkv[:, w:2 * w], cs], axis=1)
    qt = _dot_nt(pq_ref[...], xq)
    feat = lax.broadcasted_iota(jnp.int32, (qt.shape[0], 1), 0) & (LANES - 1)
    qt = qt + jnp.where((feat >= hd + 3) & (feat < hd + 6), 1.0, 0.0)
    for h in range(FOX_HEADS):
        qt_ref[h] = qt[h * LANES:(h + 1) * LANES, :].astype(BF16)
    ka = _dot(xk, pk_ref[...])
    col = lax.broadcasted_iota(jnp.int32, (1, ka.shape[1]), 1) & (LANES - 1)
    k_ref[...] = (ka + jnp.where((col >= hd) & (col < hd + 3), 1.0, 0.0)).astype(BF16)
    vt = _dot_nt(eye_ref[...], qkv[:, 2 * w:])
    for hp in range(FOX_HEADS // 2):
        vt_ref[hp] = vt[hp * LANES:(hp + 1) * LANES, :].astype(BF16)


def _attn_prep(qkv, c, bsz, seq):
    nh = FOX_HEADS
    tb = min(TQ_ATTN, seq)
    ns = seq // tb
    pq, pk, eye = _placement_matrices()
    const = lambda b, s: (0, 0)
    return pl.pallas_call(
        _attn_prep_kernel,
        grid=(bsz, ns),
        in_specs=[pl.BlockSpec((tb, qkv.shape[1]), lambda b, s: (b * ns + s, 0)),
                  pl.BlockSpec((tb, LANES), lambda b, s: (b * ns + s, 0)),
                  pl.BlockSpec(pq.shape, const), pl.BlockSpec(pk.shape, const), pl.BlockSpec(eye.shape, const)],
        out_specs=[pl.BlockSpec((None, nh, LANES, tb), lambda b, s: (b, 0, 0, s)),
                   pl.BlockSpec((tb, nh * LANES), lambda b, s: (b * ns + s, 0)),
                   pl.BlockSpec((None, nh // 2, LANES, tb), lambda b, s: (b, 0, 0, s))],
        out_shape=[jax.ShapeDtypeStruct((bsz, nh, LANES, seq), BF16),
                   jax.ShapeDtypeStruct((bsz * seq, nh * LANES), BF16),
                   jax.ShapeDtypeStruct((bsz, nh // 2, LANES, seq), BF16)],
        compiler_params=_cparams("parallel", "parallel"),
        name="attn_prep",
    )(qkv, c, pq, pk, eye)


def _attn_kernel(qt_ref, k_ref, vt_ref, o_ref, m_sc, l_sc, acc_sc):
    qi = pl.program_id(2)
    ki = pl.program_id(3)

    @pl.when(ki == 0)
    def _():
        m_sc[...] = jnp.full_like(m_sc, NEG_BIG)
        l_sc[...] = jnp.zeros_like(l_sc)
        acc_sc[...] = jnp.zeros_like(acc_sc)

    def step(masked):
        vt = vt_ref[...]
        for hh in range(2):
            s = _dot(k_ref[:, hh * LANES:(hh + 1) * LANES], qt_ref[hh])
            if masked:
                key = lax.broadcasted_iota(jnp.int32, s.shape, 0)
                qry = lax.broadcasted_iota(jnp.int32, s.shape, 1)
                s = jnp.where(key <= qry, s, NEG_BIG)
            m_prev = m_sc[hh]
            m_new = jnp.maximum(m_prev, jnp.max(s, axis=0, keepdims=True))
            alpha = jnp.exp(m_prev - m_new)
            p = jnp.exp(s - m_new)
            l_sc[hh] = alpha * l_sc[hh] + jnp.sum(p, axis=0, keepdims=True)
            acc_sc[hh] = alpha * acc_sc[hh] + _dot(vt, p.astype(BF16))
            m_sc[hh] = m_new

    @pl.when(ki < qi)
    def _():
        step(False)

    @pl.when(ki == qi)
    def _():
        step(True)
        feat = lax.broadcasted_iota(jnp.int32, (LANES, 1), 0)
        ot = jnp.where(feat < FOX_HEAD_DIM, acc_sc[0] / l_sc[0], acc_sc[1] / l_sc[1])
        o_ref[...] = ot.T.astype(o_ref.dtype)


def _attn(qt, k_aug, vt, bsz, seq):
    tq = min(TQ_ATTN, seq)
    nq = seq // tq
    npair = FOX_HEADS // 2
    kv = lambda ki, qi: jnp.minimum(ki, qi)
    return pl.pallas_call(
        _attn_kernel,
        grid=(bsz, npair, nq, nq),
        in_specs=[pl.BlockSpec((None, 2, LANES, tq), lambda b, h, qi, ki: (b, h, 0, qi)),
                  pl.BlockSpec((tq, 2 * LANES), lambda b, h, qi, ki: (b * nq + kv(ki, qi), h)),
                  pl.BlockSpec((None, None, LANES, tq), lambda b, h, qi, ki: (b, h, 0, kv(ki, qi)))],
        out_specs=pl.BlockSpec((tq, LANES), lambda b, h, qi, ki: (b * nq + qi, h)),
        out_shape=jax.ShapeDtypeStruct((bsz * seq, FOX_HEADS * FOX_HEAD_DIM), BF16),
        scratch_shapes=[pltpu.VMEM((2, 1, tq), F32), pltpu.VMEM((2, 1, tq), F32),
                        pltpu.VMEM((2, LANES, tq), F32)],
        compiler_params=_cparams("parallel", "parallel", "parallel", "arbitrary"),
        name="attn",
    )(qt, k_aug, vt)


def _swiglu(xn, w1_ref, w3_ref, w2_ref, h_sc, fc):
    f = w1_ref.shape[1]
    for c0 in range(0, f, fc):
        a = _dot(xn, w1_ref[:, c0:c0 + fc])
        b = _dot(xn, w3_ref[:, c0:c0 + fc])
        h_sc[:, c0:c0 + fc] = (a * _sigmoid(a) * b).astype(BF16)
    return _dot(h_sc[...], w2_ref[...])


def _ffn_kernel(h_ref, ya_ref, yb_ref, wo_ref, g_ref, w1_ref, w3_ref, w2_ref, o_ref, h_sc):
    half = ya_ref.shape[1]
    h1 = h_ref[...] + _dot(ya_ref[...], wo_ref[:half, :]) + _dot(yb_ref[...], wo_ref[half:, :])
    xn = _rmsnorm(h1, g_ref[...]).astype(BF16)
    o_ref[...] = h1 + _swiglu(xn, w1_ref, w3_ref, w2_ref, h_sc, FC_FFN)


def _ffn(h, ya, yb, wo, g, w1, w3, w2):
    t, d = h.shape
    tm = TM_FFN
    f = w1.shape[1]
    half = ya.shape[1]
    row = lambda i: (i, 0)
    c2 = lambda i: (0, 0)
    return pl.pallas_call(
        _ffn_kernel,
        grid=(t // tm,),
        in_specs=[pl.BlockSpec((tm, d), row), pl.BlockSpec((tm, half), row), pl.BlockSpec((tm, half), row),
                  pl.BlockSpec((d, d), c2), pl.BlockSpec((1, d), c2),
                  pl.BlockSpec((d, f), c2), pl.BlockSpec((d, f), c2), pl.BlockSpec((f, d), c2)],
        out_specs=pl.BlockSpec((tm, d), row),
        out_shape=jax.ShapeDtypeStruct((t, d), F32),
        scratch_shapes=[pltpu.VMEM((tm, f), BF16)],
        compiler_params=_cparams("parallel"),
        name="ffn",
    )(h, ya, yb, wo, g, w1, w3, w2)


def _gla_kernel(q_ref, k_ref, v_ref, g_ref, lr_ref, gw_ref, gb_ref, hn_ref, o_ref, state_t, *, rows):
    @pl.when(pl.program_id(2) == 0)
    def _():
        state_t[...] = jnp.zeros_like(state_t)

    cs = GLA_CHUNK
    log_a = _log_sigmoid(_dot(lr_ref[...].astype(BF16), gw_ref[...]) + gb_ref[...]) * (1.0 / GLA_TAU)
    rowi = lax.broadcasted_iota(jnp.int32, (cs, GLA_DK), 0)
    tri = (lax.broadcasted_iota(jnp.int32, (cs, cs), 0) >= lax.broadcasted_iota(jnp.int32, (cs, cs), 1))
    for ci in range(rows // cs):
        sl = slice(ci * cs, (ci + 1) * cs)
        b = log_a[sl]
        d = 1
        while d < cs:
            b = b + jnp.where(rowi >= d, pltpu.roll(b, d, 0), 0.0)
            d *= 2
        b_last = b[cs - 1:cs, :]
        q = q_ref[sl, :] * (GLA_DK ** -0.5)
        k = k_ref[sl, :]
        v = v_ref[sl, :]
        q_dec = (q * jnp.exp(b)).astype(BF16)
        k_intra = (k * jnp.exp(-b)).astype(BF16)
        k_state = (k * jnp.exp(b_last - b)).astype(BF16)
        scores = jnp.where(tri, _dot_nt(q_dec, k_intra), 0.0)
        st = state_t[...]
        o = _dot(scores.astype(BF16), v) + _dot_nt(q_dec, st.astype(BF16))
        state_t[...] = st * jnp.exp(b_last) + _dot_tn(v, k_state)
        o = o * lax.rsqrt(jnp.mean(o * o, axis=-1, keepdims=True) + EPS) * hn_ref[...]
        g = g_ref[sl, :]
        o_ref[sl, :] = (o * (g * _sigmoid(g))).astype(o_ref.dtype)


def _gla(qk, v, g, lr, gw, gb, hn, bsz, seq):
    t = qk.shape[0]
    rows = R_GLA
    nr = seq // rows
    nh = GLA_HEADS
    return pl.pallas_call(
        functools.partial(_gla_kernel, rows=rows),
        grid=(bsz, nh, nr),
        in_specs=[pl.BlockSpec((rows, GLA_DK), lambda b, h, r: (b * nr + r, h)),
                  pl.BlockSpec((rows, GLA_DK), lambda b, h, r: (b * nr + r, nh + h)),
                  pl.BlockSpec((rows, GLA_DV), lambda b, h, r: (b * nr + r, h)),
                  pl.BlockSpec((rows, GLA_DV), lambda b, h, r: (b * nr + r, h)),
                  pl.BlockSpec((rows, LANES), lambda b, h, r: (b * nr + r, 0)),
                  pl.BlockSpec((LANES, GLA_DK), lambda b, h, r: (0, h)),
                  pl.BlockSpec((1, GLA_DK), lambda b, h, r: (0, h)),
                  pl.BlockSpec((1, GLA_DV), lambda b, h, r: (0, h))],
        out_specs=pl.BlockSpec((rows, GLA_DV), lambda b, h, r: (b * nr + r, h)),
        out_shape=jax.ShapeDtypeStruct((t, nh * GLA_DV), BF16),
        scratch_shapes=[pltpu.VMEM((GLA_DV, GLA_DK), F32)],
        compiler_params=_cparams("parallel", "parallel", "arbitrary"),
        name="gla",
    )(qk, qk, v, g, lr, gw, gb, hn)


def _router_kernel(h_ref, o_ref, wo_ref, g_ref, rw_ref, h1_ref, xn_ref, info_ref):
    h1 = h_ref[...] + _dot(o_ref[...], wo_ref[...])
    h1_ref[...] = h1
    xn = _rmsnorm(h1, g_ref[...])
    xn_ref[...] = xn.astype(xn_ref.dtype)
    logits = jnp.dot(xn, rw_ref[...], preferred_element_type=F32, precision=lax.Precision.HIGHEST)
    lane = lax.broadcasted_iota(jnp.int32, logits.shape, 1)
    lg = jnp.where(lane < N_EXPERTS, logits, -jnp.inf)
    m1 = jnp.max(lg, axis=-1, keepdims=True)
    i1 = jnp.min(jnp.where(lg == m1, lane, LANES), axis=-1, keepdims=True)
    lg2 = jnp.where(lane == i1, -jnp.inf, lg)
    m2 = jnp.max(lg2, axis=-1, keepdims=True)
    i2 = jnp.min(jnp.where(lg2 == m2, lane, LANES), axis=-1, keepdims=True)
    e = jnp.exp(m2 - m1)
    w1 = 1.0 / (1.0 + e)
    w2 = e * w1
    info = jnp.where(lane == 0, i1.astype(F32),
                     jnp.where(lane == 1, i2.astype(F32),
                               jnp.where(lane == 2, w1, jnp.where(lane == 3, w2, 0.0))))
    info_ref[...] = info


def _router(h, o, wo, g, rw):
    t, d = h.shape
    tm = TM_ROUTER
    row = lambda i: (i, 0)
    c2 = lambda i: (0, 0)
    return pl.pallas_call(
        _router_kernel,
        grid=(t // tm,),
        in_specs=[pl.BlockSpec((tm, d), row), pl.BlockSpec((tm, d), row), pl.BlockSpec((d, d), c2),
                  pl.BlockSpec((1, d), c2), pl.BlockSpec((d, LANES), c2)],
        out_specs=[pl.BlockSpec((tm, d), row), pl.BlockSpec((tm, d), row), pl.BlockSpec((tm, LANES), row)],
        out_shape=[jax.ShapeDtypeStruct((t, d), F32), jax.ShapeDtypeStruct((t, d), BF16),
                   jax.ShapeDtypeStruct((t, LANES), F32)],
        compiler_params=_cparams("parallel"),
        name="router",
    )(h, o, wo, g, rw)


def _moe_kernel(te_ref, nu_ref, xs_ref, w1_ref, w3_ref, w2_ref, y_ref, h_sc):
    i = pl.program_id(0)

    @pl.when(i < nu_ref[0])
    def _():
        y_ref[...] = _swiglu(xs_ref[...], w1_ref, w3_ref, w2_ref, h_sc, FC_MOE).astype(y_ref.dtype)

    @pl.when(i >= nu_ref[0])
    def _():
        y_ref[...] = jnp.zeros_like(y_ref)


def _moe(tile_expert, n_used, xs, w1, w3, w2):
    p, d = xs.shape
    tm = TM_MOE
    f = w1.shape[2]
    grid_spec = pltpu.PrefetchScalarGridSpec(
        num_scalar_prefetch=2,
        grid=(p // tm,),
        in_specs=[pl.BlockSpec((tm, d), lambda i, te, nu: (i, 0)),
                  pl.BlockSpec((None, d, f), lambda i, te, nu: (te[i], 0, 0)),
                  pl.BlockSpec((None, d, f), lambda i, te, nu: (te[i], 0, 0)),
                  pl.BlockSpec((None, f, d), lambda i, te, nu: (te[i], 0, 0))],
        out_specs=pl.BlockSpec((tm, d), lambda i, te, nu: (i, 0)),
        scratch_shapes=[pltpu.VMEM((tm, f), BF16)],
    )
    return pl.pallas_call(
        _moe_kernel,
        grid_spec=grid_spec,
        out_shape=jax.ShapeDtypeStruct((p, d), F32),
        compiler_params=pltpu.CompilerParams(dimension_semantics=("arbitrary",), vmem_limit_bytes=VMEM_LIMIT_MOE),
        name="moe",
    )(tile_expert, n_used, xs, w1, w3, w2)


def _combine_kernel(h1_ref, y1_ref, y2_ref, info_ref, gf_ref, o_ref, *, final):
    info = info_ref[...]
    out = h1_ref[...] + info[:, 2:3] * y1_ref[...] + info[:, 3:4] * y2_ref[...]
    if final:
        out = _rmsnorm(out, gf_ref[...])
    o_ref[...] = out


def _combine(h1, yg, info, gf, final):
    t, d = h1.shape
    tm = TM_COMBINE
    nt = t // tm
    return pl.pallas_call(
        functools.partial(_combine_kernel, final=final),
        grid=(nt,),
        in_specs=[pl.BlockSpec((tm, d), lambda i: (i, 0)),
                  pl.BlockSpec((tm, d), lambda i: (i, 0)),
                  pl.BlockSpec((tm, d), lambda i: (nt + i, 0)),
                  pl.BlockSpec((tm, LANES), lambda i: (i, 0)),
                  pl.BlockSpec((1, d), lambda i: (0, 0))],
        out_specs=pl.BlockSpec((tm, d), lambda i: (i, 0)),
        out_shape=jax.ShapeDtypeStruct((t, d), F32),
        compiler_params=_cparams("parallel"),
        name="combine",
    )(h1, yg, yg, info, gf)


def _routing_plan(info, t):
    tm = TM_MOE
    n_tiles = (2 * t) // tm + N_EXPERTS
    top = jnp.concatenate([info[:, 0], info[:, 1]]).astype(jnp.int32)
    onehot = (top[:, None] == jnp.arange(N_EXPERTS, dtype=jnp.int32)[None, :]).astype(jnp.int32)
    csum = jnp.cumsum(onehot, axis=0)
    rank = jnp.sum((csum - onehot) * onehot, axis=1)
    counts = csum[-1]
    ntile = (counts + tm - 1) // tm
    tile_end = jnp.cumsum(ntile)
    tile_start = tile_end - ntile
    pos = tile_start[top] * tm + rank
    n_used = tile_end[-1]
    tile_ids = jnp.arange(n_tiles, dtype=jnp.int32)
    te = jnp.sum((tile_ids[:, None] >= tile_end[None, :]).astype(jnp.int32), axis=1)
    last = jnp.sum((n_used - 1 >= tile_end).astype(jnp.int32))
    te = jnp.where(tile_ids < n_used, te, last).astype(jnp.int32)
    tok = jnp.arange(t, dtype=jnp.int32)
    token_of = jnp.zeros((n_tiles * tm,), jnp.int32).at[pos].set(jnp.concatenate([tok, tok]))
    return te, n_used.reshape(1).astype(jnp.int32), token_of, pos


def _pad_cols(w, n):
    return jnp.pad(w, ((0, 0), (0, n - w.shape[1])))


def _block_diag(w):
    nb, bs, _ = w.shape
    eye = jnp.eye(nb, dtype=w.dtype)
    return jnp.einsum('nij,nm->nimj', w, eye).reshape(nb * bs, nb * bs)


def kernel(x, norm_mix, norm_ffn, norm_final, ev_w_in, ev_conv_w, ev_conv_b, ev_ga_w, ev_ga_b, ev_gx_w, ev_gx_b, ev_lambda, ev_f_b, ev_w_out, ev_ffn_w1, ev_ffn_w3, ev_ffn_w2, od_w_in, od_gate_w2, od_gate_b, od_head_norm, od_w_out, od_router, od_exp_w1, od_exp_w3, od_exp_w2):
    bsz, seq, d = x.shape
    t = bsz * seq
    depth = norm_mix.shape[0]
    h = x.reshape(t, d)
    n_even_main = 2 * LRU_WIDTH + 3 * FOX_HEADS * FOX_HEAD_DIM
    n_odd_main = 2 * GLA_HEADS * GLA_DK + 2 * GLA_HEADS * GLA_DV
    row = lambda v: v.reshape(1, -1)

    for layer in range(depth):
        j = layer // 2
        g_mix = row(norm_mix[layer])
        g_ffn = row(norm_ffn[layer])
        if layer % 2 == 0:
            w_in = ev_w_in[j]
            w = jnp.concatenate([w_in[:, :n_even_main], _pad_cols(w_in[:, n_even_main:], LANES)], axis=1).astype(BF16)
            xy, qkv, f = _proj(h, g_mix, w, [(0, 2 * LRU_WIDTH, F32), (2 * LRU_WIDTH, 1536, BF16), (n_even_main, LANES, F32)])
            fb = _pad_cols(row(ev_f_b[j]), LANES)
            ya, c = _lru(xy, f, ev_conv_w[j], row(ev_conv_b[j]),
                         _block_diag(ev_ga_w[j]).astype(BF16), _block_diag(ev_gx_w[j]).astype(BF16),
                         row(ev_ga_b[j]), row(ev_gx_b[j]), row(ev_lambda[j]), fb, bsz, seq)
            yb = _attn(*_attn_prep(qkv, c, bsz, seq), bsz, seq)
            h = _ffn(h, ya, yb, ev_w_out[j].astype(BF16), g_ffn,
                     ev_ffn_w1[j].astype(BF16), ev_ffn_w3[j].astype(BF16), ev_ffn_w2[j].astype(BF16))
        else:
            w_in = od_w_in[j]
            w = jnp.concatenate([w_in[:, :n_odd_main], _pad_cols(w_in[:, n_odd_main:], LANES)], axis=1).astype(BF16)
            qk, v, g, lr = _proj(h, g_mix, w, [(0, 1024, F32), (1024, 1024, BF16), (2048, 1024, F32), (n_odd_main, LANES, F32)])
            gw = jnp.pad(od_gate_w2[j], ((0, LANES - GLA_RANK), (0, 0))).astype(BF16)
            o = _gla(qk, v, g, lr, gw, row(od_gate_b[j]), row(od_head_norm[j]), bsz, seq)
            rw = _pad_cols(od_router[j], LANES)
            h1, xn, info = _router(h, o, od_w_out[j].astype(BF16), g_ffn, rw)
            te, n_used, token_of, pos = _routing_plan(info, t)
            xs = jnp.take(xn, token_of, axis=0)
            y = _moe(te, n_used, xs, od_exp_w1[j].astype(BF16), od_exp_w3[j].astype(BF16), od_exp_w2[j].astype(BF16))
            yg = jnp.take(y, pos, axis=0)
            final = layer == depth - 1
            h = _combine(h1, yg, info, row(norm_final), final)
    if depth % 2 == 1:
        raise NotImplementedError("final norm is fused into the last odd layer")
    return h.reshape(bsz, seq, d)
```

```python
import functools

import jax
import jax.numpy as jnp
import numpy as np
from jax import lax
from jax.experimental import pallas as pl
from jax.experimental.pallas import tpu as pltpu
from jax.experimental.pallas import tpu_sc as plsc

F32 = jnp.float32
BF16 = jnp.bfloat16

EPS = 1e-6
D_MODEL = 1024
LRU_WIDTH = 512
LRU_BLOCKS = 8
CONV_W = 4
LRU_C = 8.0
FOX_HEADS = 8
FOX_HEAD_DIM = 64
GLA_HEADS = 4
GLA_DK = 128
GLA_DV = 256
GLA_RANK = 16
GLA_TAU = 16.0
GLA_CHUNK = 64
N_EXPERTS = 8
LANES = 128
SUBLANES = 8
NEG_BIG = -1e30
VMEM_LIMIT = 56 * 1024 * 1024
VMEM_LIMIT_MOE = 60 * 1024 * 1024

TM_PROJ = 512
TC_LRU = 256
TQ_ATTN = 512
TM_FFN = 512
FC_FFN = 256
R_GLA = 1024
TM_ROUTER = 512
TM_MOE = 256
FC_MOE = 512
TM_COMBINE = 512
SC_GATHER_WINDOW = 128


def _cparams(*sem):
    return pltpu.CompilerParams(dimension_semantics=sem, vmem_limit_bytes=VMEM_LIMIT)


def _rmsnorm(x, g):
    return x * lax.rsqrt(jnp.mean(x * x, axis=-1, keepdims=True) + EPS) * g


def _sigmoid(x):
    return 1.0 / (1.0 + jnp.exp(-x))


def _log_sigmoid(z):
    return jnp.minimum(z, 0.0) - jnp.log1p(jnp.exp(-jnp.abs(z)))


def _dot(a, b):
    return jnp.dot(a, b, preferred_element_type=F32)


def _dot_nt(a, b):
    return lax.dot_general(a, b, (((1,), (1,)), ((), ())), preferred_element_type=F32)


def _dot_tn(a, b):
    return lax.dot_general(a, b, (((0,), (0,)), ((), ())), preferred_element_type=F32)


def _proj_kernel(h_ref, g_ref, w_ref, *out_refs, splits):
    xn = _rmsnorm(h_ref[...], g_ref[...]).astype(BF16)
    for o_ref, (start, width) in zip(out_refs, splits):
        o_ref[...] = _dot(xn, w_ref[:, start:start + width]).astype(o_ref.dtype)


def _proj(h, g, w, outs):
    t, d = h.shape
    n = w.shape[1]
    tm = TM_PROJ
    splits = tuple((s, wd) for s, wd, _ in outs)
    return pl.pallas_call(
        functools.partial(_proj_kernel, splits=splits),
        grid=(t // tm,),
        in_specs=[pl.BlockSpec((tm, d), lambda i: (i, 0)),
                  pl.BlockSpec((1, d), lambda i: (0, 0)),
                  pl.BlockSpec((d, n), lambda i: (0, 0))],
        out_specs=[pl.BlockSpec((tm, wd), lambda i: (i, 0)) for _, wd, _ in outs],
        out_shape=[jax.ShapeDtypeStruct((t, wd), dt) for _, wd, dt in outs],
        compiler_params=_cparams("parallel"),
        name="proj",
    )(h, g, w)


def _scan8(a, u):
    row = lax.broadcasted_iota(jnp.int32, u.shape, 0) & (SUBLANES - 1)
    for d in (1, 2, 4):
        m = row >= d
        u_sh = pltpu.roll(u, d, 0)
        if a is None:
            u = jnp.where(m, u + u_sh, u)
        else:
            a_sh = pltpu.roll(a, d, 0)
            u = jnp.where(m, a * u_sh + u, u)
            a = jnp.where(m, a * a_sh, a)
    return a, u


def _lru_kernel(xr_ref, yr_ref, f_ref, cw_ref, cb_ref, wga_ref, wgx_ref, gab_ref, gxb_ref,
                lam_ref, fb_ref, ya_ref, c_ref, xprev, hprev, cprev, hs, *, tc):
    @pl.when(pl.program_id(1) == 0)
    def _():
        xprev[...] = jnp.zeros_like(xprev)
        hprev[...] = jnp.zeros_like(hprev)
        cprev[...] = jnp.zeros_like(cprev)

    x = xr_ref[...]
    p8 = xprev[...]
    row8 = lax.broadcasted_iota(jnp.int32, p8.shape, 0)
    xc = cb_ref[...] + cw_ref[CONV_W - 1:CONV_W, :] * x
    for j in range(CONV_W - 1):
        sh = CONV_W - 1 - j
        xs = pltpu.roll(x, sh, 0)
        head = jnp.where(row8 < sh, pltpu.roll(p8, sh, 0), xs[:SUBLANES])
        xs = jnp.concatenate([head, xs[SUBLANES:]], axis=0)
        xc = xc + cw_ref[j:j + 1, :] * xs
    xprev[...] = x[tc - SUBLANES:, :]

    xb = xc.astype(BF16)
    r = _sigmoid(_dot(xb, wga_ref[...]) + gab_ref[...])
    i = _sigmoid(_dot(xb, wgx_ref[...]) + gxb_ref[...])
    nl = -lam_ref[...]
    softplus_nl = jnp.maximum(nl, 0.0) + jnp.log1p(jnp.exp(-jnp.abs(nl)))
    log_a = (-LRU_C) * r * softplus_nl
    a = jnp.exp(log_a)
    u = jnp.sqrt(-jnp.tanh(log_a) * (1.0 + a * a)) * (i * xc)

    a8, u8 = _scan8(a, u)
    _, lf8 = _scan8(None, _log_sigmoid(f_ref[...] + fb_ref[...]))

    h = hprev[...]
    c = cprev[...]
    for gi in range(tc // SUBLANES):
        rows = slice(gi * SUBLANES, (gi + 1) * SUBLANES)
        blk = a8[rows] * h + u8[rows]
        hs[rows, :] = blk
        h = blk[SUBLANES - 1:SUBLANES, :]
        cblk = lf8[rows] + c
        c_ref[rows, :] = cblk
        c = cblk[SUBLANES - 1:SUBLANES, :]
    hprev[...] = h
    cprev[...] = c

    yr = yr_ref[...]
    gelu = 0.5 * yr * (1.0 + jnp.tanh(0.7978845608028654 * (yr + 0.044715 * (yr * yr * yr))))
    ya_ref[...] = (hs[...] * gelu).astype(BF16)


def _lru(xy, f, cw, cb, wga, wgx, gab, gxb, lam, fb, bsz, seq):
    t = xy.shape[0]
    w = LRU_WIDTH
    tc = TC_LRU
    ns = seq // tc
    row = lambda b, s: (b * ns + s, 0)
    const = lambda b, s: (0, 0)
    return pl.pallas_call(
        functools.partial(_lru_kernel, tc=tc),
        grid=(bsz, ns),
        in_specs=[pl.BlockSpec((tc, w), row),
                  pl.BlockSpec((tc, w), lambda b, s: (b * ns + s, 1)),
                  pl.BlockSpec((tc, LANES), row),
                  pl.BlockSpec((CONV_W, w), const),
                  pl.BlockSpec((1, w), const),
                  pl.BlockSpec((w, w), const),
                  pl.BlockSpec((w, w), const),
                  pl.BlockSpec((1, w), const),
                  pl.BlockSpec((1, w), const),
                  pl.BlockSpec((1, w), const),
                  pl.BlockSpec((1, LANES), const)],
        out_specs=[pl.BlockSpec((tc, w), row), pl.BlockSpec((tc, LANES), row)],
        out_shape=[jax.ShapeDtypeStruct((t, w), BF16), jax.ShapeDtypeStruct((t, LANES), F32)],
        scratch_shapes=[pltpu.VMEM((SUBLANES, w), F32), pltpu.VMEM((1, w), F32),
                        pltpu.VMEM((1, LANES), F32), pltpu.VMEM((tc, w), F32)],
        compiler_params=_cparams("parallel", "arbitrary"),
        name="lru",
    )(xy, xy, f, cw, cb, wga, wgx, gab, gxb, lam, fb)


def _placement_matrices():
    nh, hd = FOX_HEADS, FOX_HEAD_DIM
    w = nh * hd
    pq = np.zeros((nh * LANES, w + 3 * LANES), np.float32)
    pk = np.zeros((w + 3 * LANES, nh * LANES), np.float32)
    for h in range(nh):
        for j in range(hd):
            pq[h * LANES + j, h * hd + j] = hd ** -0.5
            pk[h * hd + j, h * LANES + j] = 1.0
        for part in range(3):
            pq[h * LANES + hd + part, w + part * LANES + h] = 1.0
            pk[w + part * LANES + h, h * LANES + hd + 3 + part] = -1.0
    return jnp.asarray(pq, BF16), jnp.asarray(pk, BF16), jnp.eye(w, dtype=BF16)


def _attn_prep_kernel(qkv_ref, c_ref, pq_ref, pk_ref, eye_ref, qt_ref, k_ref, vt_ref):
    hd = FOX_HEAD_DIM
    w = FOX_HEADS * hd
    c = c_ref[...]
    hi = c.astype(BF16)
    r1 = c - hi.astype(F32)
    mid = r1.astype(BF16)
    lo = (r1 - mid.astype(F32)).astype(BF16)
    cs = jnp.concatenate([hi, mid, lo], axis=1)
    qkv = qkv_ref[...]
    xq = jnp.concatenate([qkv[:, :w], cs], axis=1)
    xk = jnp.concatenate([qkv[:, w:2 * w], cs], axis=1)
    qt = _dot_nt(pq_ref[...], xq)
    feat = lax.broadcasted_iota(jnp.int32, (qt.shape[0], 1), 0) & (LANES - 1)
    qt = qt + jnp.where((feat >= hd + 3) & (feat < hd + 6), 1.0, 0.0)
    for h in range(FOX_HEADS):
        qt_ref[h] = qt[h * LANES:(h + 1) * LANES, :].astype(BF16)
    ka = _dot(xk, pk_ref[...])
    col = lax.broadcasted_iota(jnp.int32, (1, ka.shape[1]), 1) & (LANES - 1)
    k_ref[...] = (ka + jnp.where((col >= hd) & (col < hd + 3), 1.0, 0.0)).astype(BF16)
    vt = _dot_nt(eye_ref[...], qkv[:, 2 * w:])
    for hp in range(FOX_HEADS // 2):
        vt_ref[hp, 0] = vt[hp * LANES:(hp + 1) * LANES, :].astype(BF16)


def _attn_prep(qkv, c, bsz, seq):
    nh = FOX_HEADS
    tb = min(TQ_ATTN, seq)
    ns = seq // tb
    pq, pk, eye = _placement_matrices()
    const = lambda b, s: (0, 0)
    return pl.pallas_call(
        _attn_prep_kernel,
        grid=(bsz, ns),
        in_specs=[pl.BlockSpec((tb, qkv.shape[1]), lambda b, s: (b * ns + s, 0)),
                  pl.BlockSpec((tb, LANES), lambda b, s: (b * ns + s, 0)),
                  pl.BlockSpec(pq.shape, const), pl.BlockSpec(pk.shape, const), pl.BlockSpec(eye.shape, const)],
        out_specs=[pl.BlockSpec((None, nh, LANES, tb), lambda b, s: (b, 0, 0, s)),
                   pl.BlockSpec((tb, nh * LANES), lambda b, s: (b * ns + s, 0)),
                   pl.BlockSpec((None, nh // 2, 1, LANES, tb), lambda b, s: (b, 0, s, 0, 0))],
        out_shape=[jax.ShapeDtypeStruct((bsz, nh, LANES, seq), BF16),
                   jax.ShapeDtypeStruct((bsz * seq, nh * LANES), BF16),
                   jax.ShapeDtypeStruct((bsz, nh // 2, ns, LANES, tb), BF16)],
        compiler_params=_cparams("parallel", "parallel"),
        name="attn_prep",
    )(qkv, c, pq, pk, eye)


def _attn_kernel(qt_ref, k_ref, vt_ref, o_ref, m_sc, l_sc, acc_sc, *, tk):
    qi = pl.program_id(2)
    m_sc[...] = jnp.full_like(m_sc, NEG_BIG)
    l_sc[...] = jnp.zeros_like(l_sc)
    acc_sc[...] = jnp.zeros_like(acc_sc)

    def step(kb, masked):
        vt = vt_ref[kb]
        k2 = k_ref[pl.ds(pl.multiple_of(kb * tk, tk), tk), :]
        for hh in range(2):
            s = _dot(k2[:, hh * LANES:(hh + 1) * LANES], qt_ref[hh])
            if masked:
                key = lax.broadcasted_iota(jnp.int32, s.shape, 0)
                qry = lax.broadcasted_iota(jnp.int32, s.shape, 1)
                s = jnp.where(key <= qry, s, NEG_BIG)
            m_prev = m_sc[hh]
            m_new = jnp.maximum(m_prev, jnp.max(s, axis=0, keepdims=True))
            alpha = jnp.exp(m_prev - m_new)
            p = jnp.exp(s - m_new)
            l_sc[hh] = alpha * l_sc[hh] + jnp.sum(p, axis=0, keepdims=True)
            acc_sc[hh] = alpha * acc_sc[hh] + _dot(vt, p.astype(BF16))
            m_sc[hh] = m_new

    def body(kb, carry):
        step(kb, False)
        return carry

    lax.fori_loop(0, qi, body, 0)
    step(qi, True)
    feat = lax.broadcasted_iota(jnp.int32, (LANES, 1), 0)
    ot = jnp.where(feat < FOX_HEAD_DIM, acc_sc[0] / l_sc[0], acc_sc[1] / l_sc[1])
    o_ref[...] = ot.T.astype(o_ref.dtype)


def _attn(qt, k_aug, vt, bsz, seq):
    nk, tk = vt.shape[2], vt.shape[4]
    tq = tk
    nq = seq // tq
    npair = FOX_HEADS // 2
    return pl.pallas_call(
        functools.partial(_attn_kernel, tk=tk),
        grid=(bsz, npair, nq),
        in_specs=[pl.BlockSpec((None, 2, LANES, tq), lambda b, h, qi: (b, h, 0, qi)),
                  pl.BlockSpec((seq, 2 * LANES), lambda b, h, qi: (b, h)),
                  pl.BlockSpec((None, None, nk, LANES, tk), lambda b, h, qi: (b, h, 0, 0, 0))],
        out_specs=pl.BlockSpec((tq, LANES), lambda b, h, qi: (b * nq + qi, h)),
        out_shape=jax.ShapeDtypeStruct((bsz * seq, FOX_HEADS * FOX_HEAD_DIM), BF16),
        scratch_shapes=[pltpu.VMEM((2, 1, tq), F32), pltpu.VMEM((2, 1, tq), F32),
                        pltpu.VMEM((2, LANES, tq), F32)],
        compiler_params=_cparams("parallel", "parallel", "arbitrary"),
        name="attn",
    )(qt, k_aug, vt)


def _swiglu(xn, w1_ref, w3_ref, w2_ref, h_sc, fc):
    f = w1_ref.shape[1]
    for c0 in range(0, f, fc):
        a = _dot(xn, w1_ref[:, c0:c0 + fc])
        b = _dot(xn, w3_ref[:, c0:c0 + fc])
        h_sc[:, c0:c0 + fc] = (a * _sigmoid(a) * b).astype(BF16)
    return _dot(h_sc[...], w2_ref[...])


def _ffn_kernel(h_ref, ya_ref, yb_ref, wo_ref, g_ref, w1_ref, w3_ref, w2_ref, o_ref, h_sc):
    half = ya_ref.shape[1]
    h1 = h_ref[...] + _dot(ya_ref[...], wo_ref[:half, :]) + _dot(yb_ref[...], wo_ref[half:, :])
    xn = _rmsnorm(h1, g_ref[...]).astype(BF16)
    o_ref[...] = h1 + _swiglu(xn, w1_ref, w3_ref, w2_ref, h_sc, FC_FFN)


def _ffn(h, ya, yb, wo, g, w1, w3, w2):
    t, d = h.shape
    tm = TM_FFN
    f = w1.shape[1]
    half = ya.shape[1]
    row = lambda i: (i, 0)
    c2 = lambda i: (0, 0)
    return pl.pallas_call(
        _ffn_kernel,
        grid=(t // tm,),
        in_specs=[pl.BlockSpec((tm, d), row), pl.BlockSpec((tm, half), row), pl.BlockSpec((tm, half), row),
                  pl.BlockSpec((d, d), c2), pl.BlockSpec((1, d), c2),
                  pl.BlockSpec((d, f), c2), pl.BlockSpec((d, f), c2), pl.BlockSpec((f, d), c2)],
        out_specs=pl.BlockSpec((tm, d), row),
        out_shape=jax.ShapeDtypeStruct((t, d), F32),
        scratch_shapes=[pltpu.VMEM((tm, f), BF16)],
        compiler_params=_cparams("parallel"),
        name="ffn",
    )(h, ya, yb, wo, g, w1, w3, w2)


def _gla_kernel(q_ref, k_ref, v_ref, g_ref, lr_ref, gw_ref, gb_ref, hn_ref, o_ref, state_t, *, rows):
    @pl.when(pl.program_id(2) == 0)
    def _():
        state_t[...] = jnp.zeros_like(state_t)

    cs = GLA_CHUNK
    log_a = _log_sigmoid(_dot(lr_ref[...].astype(BF16), gw_ref[...]) + gb_ref[...]) * (1.0 / GLA_TAU)
    rowi = lax.broadcasted_iota(jnp.int32, (cs, GLA_DK), 0)
    tri = (lax.broadcasted_iota(jnp.int32, (cs, cs), 0) >= lax.broadcasted_iota(jnp.int32, (cs, cs), 1))
    for ci in range(rows // cs):
        sl = slice(ci * cs, (ci + 1) * cs)
        b = log_a[sl]
        d = 1
        while d < cs:
            b = b + jnp.where(rowi >= d, pltpu.roll(b, d, 0), 0.0)
            d *= 2
        b_last = b[cs - 1:cs, :]
        q = q_ref[sl, :] * (GLA_DK ** -0.5)
        k = k_ref[sl, :]
        v = v_ref[sl, :]
        q_dec = (q * jnp.exp(b)).astype(BF16)
        k_intra = (k * jnp.exp(-b)).astype(BF16)
        k_state = (k * jnp.exp(b_last - b)).astype(BF16)
        scores = jnp.where(tri, _dot_nt(q_dec, k_intra), 0.0)
        st = state_t[...]
        o = _dot(scores.astype(BF16), v) + _dot_nt(q_dec, st.astype(BF16))
        state_t[...] = st * jnp.exp(b_last) + _dot_tn(v, k_state)
        o = o * lax.rsqrt(jnp.mean(o * o, axis=-1, keepdims=True) + EPS) * hn_ref[...]
        g = g_ref[sl, :]
        o_ref[sl, :] = (o * (g * _sigmoid(g))).astype(o_ref.dtype)


def _gla(qk, v, g, lr, gw, gb, hn, bsz, seq):
    t = qk.shape[0]
    rows = min(R_GLA, seq)
    nr = seq // rows
    nh = GLA_HEADS
    return pl.pallas_call(
        functools.partial(_gla_kernel, rows=rows),
        grid=(bsz, nh, nr),
        in_specs=[pl.BlockSpec((rows, GLA_DK), lambda b, h, r: (b * nr + r, h)),
                  pl.BlockSpec((rows, GLA_DK), lambda b, h, r: (b * nr + r, nh + h)),
                  pl.BlockSpec((rows, GLA_DV), lambda b, h, r: (b * nr + r, h)),
                  pl.BlockSpec((rows, GLA_DV), lambda b, h, r: (b * nr + r, h)),
                  pl.BlockSpec((rows, LANES), lambda b, h, r: (b * nr + r, 0)),
                  pl.BlockSpec((LANES, GLA_DK), lambda b, h, r: (0, h)),
                  pl.BlockSpec((1, GLA_DK), lambda b, h, r: (0, h)),
                  pl.BlockSpec((1, GLA_DV), lambda b, h, r: (0, h))],
        out_specs=pl.BlockSpec((rows, GLA_DV), lambda b, h, r: (b * nr + r, h)),
        out_shape=jax.ShapeDtypeStruct((t, nh * GLA_DV), BF16),
        scratch_shapes=[pltpu.VMEM((GLA_DV, GLA_DK), F32)],
        compiler_params=_cparams("parallel", "parallel", "arbitrary"),
        name="gla",
    )(qk, qk, v, g, lr, gw, gb, hn)


def _router_kernel(h_ref, o_ref, wo_ref, g_ref, rw_ref, h1_ref, xn_ref, info_ref):
    h1 = h_ref[...] + _dot(o_ref[...], wo_ref[...])
    h1_ref[...] = h1
    xn = _rmsnorm(h1, g_ref[...])
    _split_lanes(xn_ref, xn)
    rw = rw_ref[...]
    xh = xn.astype(BF16)
    xl = (xn - xh.astype(F32)).astype(BF16)
    rh = rw.astype(BF16)
    rl = (rw - rh.astype(F32)).astype(BF16)
    logits = _dot(xh, rh) + (_dot(xh, rl) + _dot(xl, rh))
    lane = lax.broadcasted_iota(jnp.int32, logits.shape, 1)
    lg = jnp.where(lane < N_EXPERTS, logits, -jnp.inf)
    m1 = jnp.max(lg, axis=-1, keepdims=True)
    i1 = jnp.min(jnp.where(lg == m1, lane, LANES), axis=-1, keepdims=True)
    lg2 = jnp.where(lane == i1, -jnp.inf, lg)
    m2 = jnp.max(lg2, axis=-1, keepdims=True)
    i2 = jnp.min(jnp.where(lg2 == m2, lane, LANES), axis=-1, keepdims=True)
    e = jnp.exp(m2 - m1)
    w1 = 1.0 / (1.0 + e)
    w2 = e * w1
    info = jnp.where(lane == 0, i1.astype(F32),
                     jnp.where(lane == 1, i2.astype(F32),
                               jnp.where(lane == 2, w1, jnp.where(lane == 3, w2, 0.0))))
    info_ref[...] = info


def _router(h, o, wo, g, rw):
    t, d = h.shape
    tm = TM_ROUTER
    row = lambda i: (i, 0)
    c2 = lambda i: (0, 0)
    return pl.pallas_call(
        _router_kernel,
        grid=(t // tm,),
        in_specs=[pl.BlockSpec((tm, d), row), pl.BlockSpec((tm, d), row), pl.BlockSpec((d, d), c2),
                  pl.BlockSpec((1, d), c2), pl.BlockSpec((d, LANES), c2)],
        out_specs=[pl.BlockSpec((tm, d), row), pl.BlockSpec((d // LANES, tm, LANES), lambda i: (0, i, 0)),
                   pl.BlockSpec((tm, LANES), row)],
        out_shape=[jax.ShapeDtypeStruct((t, d), F32), jax.ShapeDtypeStruct((d // LANES, t, LANES), F32),
                   jax.ShapeDtypeStruct((t, LANES), F32)],
        compiler_params=_cparams("parallel"),
        name="router",
    )(h, o, wo, g, rw)


def _moe_kernel(te_ref, nu_ref, xs_ref, w1_ref, w3_ref, w2_ref, y_ref, h_sc):
    i = pl.program_id(0)

    @pl.when(i < nu_ref[0])
    def _():
        _split_lanes(y_ref, _swiglu(_join_lanes(xs_ref).astype(BF16), w1_ref, w3_ref, w2_ref, h_sc, FC_MOE))

    @pl.when(i >= nu_ref[0])
    def _():
        y_ref[...] = jnp.zeros_like(y_ref)


def _moe(tile_expert, n_used, xs, w1, w3, w2, layer):
    pieces, p, _ = xs.shape
    tm = TM_MOE
    d, f = w1.shape[2:]
    grid_spec = pltpu.PrefetchScalarGridSpec(
        num_scalar_prefetch=2,
        grid=(p // tm,),
        in_specs=[pl.BlockSpec((pieces, tm, LANES), lambda i, te, nu: (0, i, 0)),
                  pl.BlockSpec((None, None, d, f), lambda i, te, nu: (layer, te[i], 0, 0)),
                  pl.BlockSpec((None, None, d, f), lambda i, te, nu: (layer, te[i], 0, 0)),
                  pl.BlockSpec((None, None, f, d), lambda i, te, nu: (layer, te[i], 0, 0))],
        out_specs=pl.BlockSpec((pieces, tm, LANES), lambda i, te, nu: (0, i, 0)),
        scratch_shapes=[pltpu.VMEM((tm, f), BF16)],
    )
    return pl.pallas_call(
        _moe_kernel,
        grid_spec=grid_spec,
        out_shape=jax.ShapeDtypeStruct((pieces, p, LANES), F32),
        compiler_params=pltpu.CompilerParams(dimension_semantics=("arbitrary",), vmem_limit_bytes=VMEM_LIMIT_MOE),
        name="moe",
    )(tile_expert, n_used, xs, w1, w3, w2)


def _combine_kernel(h1_ref, y1_ref, y2_ref, info_ref, gf_ref, o_ref, *, final):
    info = info_ref[...]
    out = h1_ref[...] + info[:, 2:3] * _join_lanes(y1_ref) + info[:, 3:4] * _join_lanes(y2_ref)
    if final:
        out = _rmsnorm(out, gf_ref[...])
    o_ref[...] = out


def _combine(h1, yg, info, gf, final):
    t, d = h1.shape
    tm = TM_COMBINE
    nt = t // tm
    return pl.pallas_call(
        functools.partial(_combine_kernel, final=final),
        grid=(nt,),
        in_specs=[pl.BlockSpec((tm, d), lambda i: (i, 0)),
                  pl.BlockSpec((d // LANES, tm, LANES), lambda i: (0, i, 0)),
                  pl.BlockSpec((d // LANES, tm, LANES), lambda i: (0, nt + i, 0)),
                  pl.BlockSpec((tm, LANES), lambda i: (i, 0)),
                  pl.BlockSpec((1, d), lambda i: (0, 0))],
        out_specs=pl.BlockSpec((tm, d), lambda i: (i, 0)),
        out_shape=jax.ShapeDtypeStruct((t, d), F32),
        compiler_params=_cparams("parallel"),
        name="combine",
    )(h1, yg, yg, info, gf)


def _sc_gather(x, idx):
    pieces, rows, lanes = x.shape
    idx = (jnp.arange(pieces, dtype=jnp.int32)[:, None] * rows + idx[None, :]).reshape(-1)
    out = _sc_gather_pieces(x.reshape(pieces * rows, lanes), idx)
    return out.reshape(pieces, -1, lanes)


def _split_lanes(ref, x):
    for k in range(ref.shape[0]):
        ref[k] = x[:, k * LANES:(k + 1) * LANES].astype(ref.dtype)


def _join_lanes(ref):
    return jnp.concatenate([ref[k] for k in range(ref.shape[0])], axis=1)


def _sc_gather_pieces(x, idx):
    n = idx.shape[0]
    d = x.shape[1]
    win = SC_GATHER_WINDOW
    mesh = plsc.VectorSubcoreMesh(core_axis_name="c", subcore_axis_name="s")

    @functools.partial(pl.kernel, out_type=jax.ShapeDtypeStruct((n, d), x.dtype), mesh=mesh)
    def gather_kernel(x_hbm, i_hbm, o_hbm):
        def body(i_vmem, o_vmem):
            pltpu.sync_copy(x_hbm.at[i_vmem.at[0]], o_vmem)

        pltpu.emit_pipeline(
            body,
            grid=(n // win,),
            in_specs=[pl.BlockSpec((1, win), lambda i: (0, i))],
            out_specs=[pl.BlockSpec((win, d), lambda i: (i, 0))],
            core_axis_name=("c", "s"),
            dimension_semantics=(pltpu.PARALLEL,),
        )(i_hbm, o_hbm)

    return gather_kernel(x, idx.reshape(1, n))


def _routing_plan(info, t):
    tm = TM_MOE
    n_tiles = (2 * t) // tm + N_EXPERTS
    top = jnp.concatenate([info[:, 0], info[:, 1]]).astype(jnp.int32)
    onehot = (top[:, None] == jnp.arange(N_EXPERTS, dtype=jnp.int32)[None, :]).astype(jnp.int32)
    csum = jnp.cumsum(onehot, axis=0)
    rank = jnp.sum((csum - onehot) * onehot, axis=1)
    counts = csum[-1]
    ntile = (counts + tm - 1) // tm
    tile_end = jnp.cumsum(ntile)
    tile_start = tile_end - ntile
    pos = tile_start[top] * tm + rank
    n_used = tile_end[-1]
    tile_ids = jnp.arange(n_tiles, dtype=jnp.int32)
    te = jnp.sum((tile_ids[:, None] >= tile_end[None, :]).astype(jnp.int32), axis=1)
    last = jnp.sum((n_used - 1 >= tile_end).astype(jnp.int32))
    te = jnp.where(tile_ids < n_used, te, last).astype(jnp.int32)
    tok = jnp.arange(t, dtype=jnp.int32)
    token_of = jnp.zeros((n_tiles * tm,), jnp.int32).at[pos].set(jnp.concatenate([tok, tok]))
    return te, n_used.reshape(1).astype(jnp.int32), token_of, pos


def _pad_cols(w, n):
    return jnp.pad(w, ((0, 0), (0, n - w.shape[1])))


def _block_diag(w):
    nb, bs, _ = w.shape
    eye = jnp.eye(nb, dtype=w.dtype)
    return jnp.einsum('nij,nm->nimj', w, eye).reshape(nb * bs, nb * bs)


def kernel(x, norm_mix, norm_ffn, norm_final, ev_w_in, ev_conv_w, ev_conv_b, ev_ga_w, ev_ga_b, ev_gx_w, ev_gx_b, ev_lambda, ev_f_b, ev_w_out, ev_ffn_w1, ev_ffn_w3, ev_ffn_w2, od_w_in, od_gate_w2, od_gate_b, od_head_norm, od_w_out, od_router, od_exp_w1, od_exp_w3, od_exp_w2):
    bsz, seq, d = x.shape
    t = bsz * seq
    depth = norm_mix.shape[0]
    h = x.reshape(t, d)
    n_even_main = 2 * LRU_WIDTH + 3 * FOX_HEADS * FOX_HEAD_DIM
    n_odd_main = 2 * GLA_HEADS * GLA_DK + 2 * GLA_HEADS * GLA_DV
    row = lambda v: v.reshape(1, -1)
    exp_w1, exp_w3, exp_w2 = od_exp_w1.astype(BF16), od_exp_w3.astype(BF16), od_exp_w2.astype(BF16)

    for layer in range(depth):
        j = layer // 2
        g_mix = row(norm_mix[layer])
        g_ffn = row(norm_ffn[layer])
        if layer % 2 == 0:
            w_in = ev_w_in[j]
            w = jnp.concatenate([w_in[:, :n_even_main], _pad_cols(w_in[:, n_even_main:], LANES)], axis=1).astype(BF16)
            xy, qkv, f = _proj(h, g_mix, w, [(0, 2 * LRU_WIDTH, F32), (2 * LRU_WIDTH, 1536, BF16), (n_even_main, LANES, F32)])
            fb = _pad_cols(row(ev_f_b[j]), LANES)
            ya, c = _lru(xy, f, ev_conv_w[j], row(ev_conv_b[j]),
                         _block_diag(ev_ga_w[j]).astype(BF16), _block_diag(ev_gx_w[j]).astype(BF16),
                         row(ev_ga_b[j]), row(ev_gx_b[j]), row(ev_lambda[j]), fb, bsz, seq)
            yb = _attn(*_attn_prep(qkv, c, bsz, seq), bsz, seq)
            h = _ffn(h, ya, yb, ev_w_out[j].astype(BF16), g_ffn,
                     ev_ffn_w1[j].astype(BF16), ev_ffn_w3[j].astype(BF16), ev_ffn_w2[j].astype(BF16))
        else:
            w_in = od_w_in[j]
            w = jnp.concatenate([w_in[:, :n_odd_main], _pad_cols(w_in[:, n_odd_main:], LANES)], axis=1).astype(BF16)
            qk, v, g, lr = _proj(h, g_mix, w, [(0, 1024, F32), (1024, 1024, BF16), (2048, 1024, F32), (n_odd_main, LANES, F32)])
            gw = jnp.pad(od_gate_w2[j], ((0, LANES - GLA_RANK), (0, 0))).astype(BF16)
            o = _gla(qk, v, g, lr, gw, row(od_gate_b[j]), row(od_head_norm[j]), bsz, seq)
            rw = _pad_cols(od_router[j], LANES)
            h1, xn, info = _router(h, o, od_w_out[j].astype(BF16), g_ffn, rw)
            te, n_used, token_of, pos = _routing_plan(info, t)
            xs = _sc_gather(xn, token_of)
            y = _moe(te, n_used, xs, exp_w1, exp_w3, exp_w2, j)
            yg = _sc_gather(y, pos)
            final = layer == depth - 1
            h = _combine(h1, yg, info, row(norm_final), final)
    if depth % 2 == 1:
        raise NotImplementedError("final norm is fused into the last odd layer")
    return h.reshape(bsz, seq, d)
```

```python
import functools

import jax
import jax.numpy as jnp
import numpy as np
from jax import lax
from jax.experimental import pallas as pl
from jax.experimental.pallas import tpu as pltpu
from jax.experimental.pallas import tpu_sc as plsc

F32 = jnp.float32
BF16 = jnp.bfloat16

EPS = 1e-6
D_MODEL = 1024
LRU_WIDTH = 512
LRU_BLOCKS = 8
CONV_W = 4
LRU_C = 8.0
FOX_HEADS = 8
FOX_HEAD_DIM = 64
GLA_HEADS = 4
GLA_DK = 128
GLA_DV = 256
GLA_RANK = 16
GLA_TAU = 16.0
GLA_CHUNK = 64
N_EXPERTS = 8
LANES = 128
SUBLANES = 8
LOG2E = 1.4426950408889634
NEG_BIG = -1e30
VMEM_LIMIT = 56 * 1024 * 1024
VMEM_LIMIT_MOE = 60 * 1024 * 1024

TM_PROJ = 512
TC_LRU = 256
TQ_ATTN = 512
TM_FFN = 512
FC_FFN = 256
R_GLA = 1024
TM_ROUTER = 512
TM_MOE = 256
FC_MOE = 512
TM_COMBINE = 512
SC_GATHER_WINDOW = 128


def _cparams(*sem):
    return pltpu.CompilerParams(dimension_semantics=sem, vmem_limit_bytes=VMEM_LIMIT)


def _rmsnorm(x, g):
    return x * lax.rsqrt(jnp.mean(x * x, axis=-1, keepdims=True) + EPS) * g


def _sigmoid(x):
    return 1.0 / (1.0 + jnp.exp(-x))


def _log_sigmoid(z):
    return jnp.minimum(z, 0.0) - jnp.log1p(jnp.exp(-jnp.abs(z)))


def _dot(a, b):
    return jnp.dot(a, b, preferred_element_type=F32)


def _dot_nt(a, b):
    return lax.dot_general(a, b, (((1,), (1,)), ((), ())), preferred_element_type=F32)


def _dot_tn(a, b):
    return lax.dot_general(a, b, (((0,), (0,)), ((), ())), preferred_element_type=F32)


def _proj_kernel(h_ref, g_ref, w_ref, *out_refs, splits):
    xn = _rmsnorm(h_ref[...], g_ref[...]).astype(BF16)
    for o_ref, (start, width) in zip(out_refs, splits):
        o_ref[...] = _dot(xn, w_ref[:, start:start + width]).astype(o_ref.dtype)


def _proj(h, g, w, outs):
    t, d = h.shape
    n = w.shape[1]
    tm = TM_PROJ
    splits = tuple((s, wd) for s, wd, _ in outs)
    return pl.pallas_call(
        functools.partial(_proj_kernel, splits=splits),
        grid=(t // tm,),
        in_specs=[pl.BlockSpec((tm, d), lambda i: (i, 0)),
                  pl.BlockSpec((1, d), lambda i: (0, 0)),
                  pl.BlockSpec((d, n), lambda i: (0, 0))],
        out_specs=[pl.BlockSpec((tm, wd), lambda i: (i, 0)) for _, wd, _ in outs],
        out_shape=[jax.ShapeDtypeStruct((t, wd), dt) for _, wd, dt in outs],
        compiler_params=_cparams("parallel"),
        name="proj",
    )(h, g, w)


def _scan8(a, u):
    row = lax.broadcasted_iota(jnp.int32, u.shape, 0) & (SUBLANES - 1)
    for d in (1, 2, 4):
        m = row >= d
        u_sh = pltpu.roll(u, d, 0)
        if a is None:
            u = jnp.where(m, u + u_sh, u)
        else:
            a_sh = pltpu.roll(a, d, 0)
            u = jnp.where(m, a * u_sh + u, u)
            a = jnp.where(m, a * a_sh, a)
    return a, u


def _lru_kernel(xr_ref, yr_ref, f_ref, cw_ref, cb_ref, wga_ref, wgx_ref, gab_ref, gxb_ref,
                lam_ref, fb_ref, ya_ref, c_ref, xprev, hprev, cprev, hs, *, tc):
    @pl.when(pl.program_id(1) == 0)
    def _():
        xprev[...] = jnp.zeros_like(xprev)
        hprev[...] = jnp.zeros_like(hprev)
        cprev[...] = jnp.zeros_like(cprev)

    x = xr_ref[...]
    p8 = xprev[...]
    row8 = lax.broadcasted_iota(jnp.int32, p8.shape, 0)
    xc = cb_ref[...] + cw_ref[CONV_W - 1:CONV_W, :] * x
    for j in range(CONV_W - 1):
        sh = CONV_W - 1 - j
        xs = pltpu.roll(x, sh, 0)
        head = jnp.where(row8 < sh, pltpu.roll(p8, sh, 0), xs[:SUBLANES])
        xs = jnp.concatenate([head, xs[SUBLANES:]], axis=0)
        xc = xc + cw_ref[j:j + 1, :] * xs
    xprev[...] = x[tc - SUBLANES:, :]

    xb = xc.astype(BF16)
    r = _sigmoid(_dot(xb, wga_ref[...]) + gab_ref[...])
    i = _sigmoid(_dot(xb, wgx_ref[...]) + gxb_ref[...])
    nl = -lam_ref[...]
    softplus_nl = jnp.maximum(nl, 0.0) + jnp.log1p(jnp.exp(-jnp.abs(nl)))
    log_a = (-LRU_C) * r * softplus_nl
    a = jnp.exp(log_a)
    u = jnp.sqrt(-jnp.tanh(log_a) * (1.0 + a * a)) * (i * xc)

    a8, u8 = _scan8(a, u)
    _, lf8 = _scan8(None, _log_sigmoid(f_ref[...] + fb_ref[...]))

    h = hprev[...]
    c = cprev[...]
    for gi in range(tc // SUBLANES):
        rows = slice(gi * SUBLANES, (gi + 1) * SUBLANES)
        blk = a8[rows] * h + u8[rows]
        hs[rows, :] = blk
        h = blk[SUBLANES - 1:SUBLANES, :]
        cblk = lf8[rows] + c
        c_ref[rows, :] = cblk
        c = cblk[SUBLANES - 1:SUBLANES, :]
    hprev[...] = h
    cprev[...] = c

    yr = yr_ref[...]
    gelu = 0.5 * yr * (1.0 + jnp.tanh(0.7978845608028654 * (yr + 0.044715 * (yr * yr * yr))))
    ya_ref[...] = (hs[...] * gelu).astype(BF16)


def _lru(xy, f, cw, cb, wga, wgx, gab, gxb, lam, fb, bsz, seq):
    t = xy.shape[0]
    w = LRU_WIDTH
    tc = TC_LRU
    ns = seq // tc
    row = lambda b, s: (b * ns + s, 0)
    const = lambda b, s: (0, 0)
    return pl.pallas_call(
        functools.partial(_lru_kernel, tc=tc),
        grid=(bsz, ns),
        in_specs=[pl.BlockSpec((tc, w), row),
                  pl.BlockSpec((tc, w), lambda b, s: (b * ns + s, 1)),
                  pl.BlockSpec((tc, LANES), row),
                  pl.BlockSpec((CONV_W, w), const),
                  pl.BlockSpec((1, w), const),
                  pl.BlockSpec((w, w), const),
                  pl.BlockSpec((w, w), const),
                  pl.BlockSpec((1, w), const),
                  pl.BlockSpec((1, w), const),
                  pl.BlockSpec((1, w), const),
                  pl.BlockSpec((1, LANES), const)],
        out_specs=[pl.BlockSpec((tc, w), row), pl.BlockSpec((tc, LANES), row)],
        out_shape=[jax.ShapeDtypeStruct((t, w), BF16), jax.ShapeDtypeStruct((t, LANES), F32)],
        scratch_shapes=[pltpu.VMEM((SUBLANES, w), F32), pltpu.VMEM((1, w), F32),
                        pltpu.VMEM((1, LANES), F32), pltpu.VMEM((tc, w), F32)],
        compiler_params=_cparams("parallel", "arbitrary"),
        name="lru",
    )(xy, xy, f, cw, cb, wga, wgx, gab, gxb, lam, fb)


def _placement_matrices():
    nh, hd = FOX_HEADS, FOX_HEAD_DIM
    w = nh * hd
    pq = np.zeros((nh * LANES, w + 3 * LANES), np.float32)
    pk = np.zeros((w + 3 * LANES, nh * LANES), np.float32)
    for h in range(nh):
        for j in range(hd):
            pq[h * LANES + j, h * hd + j] = 1.0
            pk[h * hd + j, h * LANES + j] = 1.0
        for part in range(3):
            pq[h * LANES + hd + part, w + part * LANES + h] = 1.0
            pk[w + part * LANES + h, h * LANES + hd + 3 + part] = -1.0
    return jnp.asarray(pq, BF16), jnp.asarray(pk, BF16), jnp.eye(w, dtype=BF16)


def _attn_prep_kernel(qkv_ref, c_ref, pq_ref, pk_ref, eye_ref, qt_ref, k_ref, vt_ref):
    hd = FOX_HEAD_DIM
    w = FOX_HEADS * hd
    c = c_ref[...] * LOG2E
    hi = c.astype(BF16)
    r1 = c - hi.astype(F32)
    mid = r1.astype(BF16)
    lo = (r1 - mid.astype(F32)).astype(BF16)
    cs = jnp.concatenate([hi, mid, lo], axis=1)
    qkv = qkv_ref[...]
    q = (qkv[:, :w].astype(F32) * (hd ** -0.5 * LOG2E)).astype(BF16)
    xq = jnp.concatenate([q, cs], axis=1)
    xk = jnp.concatenate([qkv[:, w:2 * w], cs], axis=1)
    qt = _dot_nt(pq_ref[...], xq)
    feat = lax.broadcasted_iota(jnp.int32, (qt.shape[0], 1), 0) & (LANES - 1)
    qt = qt + jnp.where((feat >= hd + 3) & (feat < hd + 6), 1.0, 0.0)
    for h in range(FOX_HEADS):
        qt_ref[h] = qt[h * LANES:(h + 1) * LANES, :].astype(BF16)
    ka = _dot(xk, pk_ref[...])
    col = lax.broadcasted_iota(jnp.int32, (1, ka.shape[1]), 1) & (LANES - 1)
    k_ref[...] = (ka + jnp.where((col >= hd) & (col < hd + 3), 1.0, 0.0)).astype(BF16)
    vt = _dot_nt(eye_ref[...], qkv[:, 2 * w:])
    for hp in range(FOX_HEADS // 2):
        vt_ref[hp, 0] = vt[hp * LANES:(hp + 1) * LANES, :].astype(BF16)


def _attn_prep(qkv, c, bsz, seq):
    nh = FOX_HEADS
    tb = min(TQ_ATTN, seq)
    ns = seq // tb
    pq, pk, eye = _placement_matrices()
    const = lambda b, s: (0, 0)
    return pl.pallas_call(
        _attn_prep_kernel,
        grid=(bsz, ns),
        in_specs=[pl.BlockSpec((tb, qkv.shape[1]), lambda b, s: (b * ns + s, 0)),
                  pl.BlockSpec((tb, LANES), lambda b, s: (b * ns + s, 0)),
                  pl.BlockSpec(pq.shape, const), pl.BlockSpec(pk.shape, const), pl.BlockSpec(eye.shape, const)],
        out_specs=[pl.BlockSpec((None, nh, LANES, tb), lambda b, s: (b, 0, 0, s)),
                   pl.BlockSpec((tb, nh * LANES), lambda b, s: (b * ns + s, 0)),
                   pl.BlockSpec((None, nh // 2, 1, LANES, tb), lambda b, s: (b, 0, s, 0, 0))],
        out_shape=[jax.ShapeDtypeStruct((bsz, nh, LANES, seq), BF16),
                   jax.ShapeDtypeStruct((bsz * seq, nh * LANES), BF16),
                   jax.ShapeDtypeStruct((bsz, nh // 2, ns, LANES, tb), BF16)],
        compiler_params=_cparams("parallel", "parallel"),
        name="attn_prep",
    )(qkv, c, pq, pk, eye)


def _attn_kernel(qt_ref, k_ref, vt_ref, o_ref, m_sc, l_sc, acc_sc, s_sc, *, tk):
    qi = pl.program_id(2)
    m_sc[...] = jnp.full_like(m_sc, NEG_BIG)
    l_sc[...] = jnp.zeros_like(l_sc)
    acc_sc[...] = jnp.zeros_like(acc_sc)

    def logits(kb, hh):
        k2 = k_ref[pl.ds(pl.multiple_of(kb * tk, tk), tk), hh * LANES:(hh + 1) * LANES]
        return _dot(k2, qt_ref[hh])

    def update(kb, hh, s, masked):
        if masked:
            key = lax.broadcasted_iota(jnp.int32, s.shape, 0)
            qry = lax.broadcasted_iota(jnp.int32, s.shape, 1)
            s = jnp.where(key <= qry, s, NEG_BIG)
        m_prev = m_sc[hh]
        m_new = jnp.maximum(m_prev, jnp.max(s, axis=0, keepdims=True))
        alpha = jnp.exp2(m_prev - m_new)
        p = jnp.exp2(s - m_new)
        l_sc[hh] = alpha * l_sc[hh] + jnp.sum(p, axis=0, keepdims=True)
        acc_sc[hh] = alpha * acc_sc[hh] + _dot(vt_ref[kb], p.astype(BF16))
        m_sc[hh] = m_new

    for hh in range(2):
        s_sc[hh] = logits(0, hh)

    def body(kb, carry):
        for hh in range(2):
            s = s_sc[hh]
            s_next = logits(kb + 1, hh)
            update(kb, hh, s, False)
            s_sc[hh] = s_next
        return carry

    lax.fori_loop(0, qi, body, 0)
    for hh in range(2):
        update(qi, hh, s_sc[hh], True)
    feat = lax.broadcasted_iota(jnp.int32, (LANES, 1), 0)
    ot = jnp.where(feat < FOX_HEAD_DIM, acc_sc[0] / l_sc[0], acc_sc[1] / l_sc[1])
    o_ref[...] = ot.T.astype(o_ref.dtype)


def _attn(qt, k_aug, vt, bsz, seq):
    nk, tk = vt.shape[2], vt.shape[4]
    tq = tk
    nq = seq // tq
    npair = FOX_HEADS // 2
    return pl.pallas_call(
        functools.partial(_attn_kernel, tk=tk),
        grid=(bsz, npair, nq),
        in_specs=[pl.BlockSpec((None, 2, LANES, tq), lambda b, h, qi: (b, h, 0, qi)),
                  pl.BlockSpec((seq, 2 * LANES), lambda b, h, qi: (b, h)),
                  pl.BlockSpec((None, None, nk, LANES, tk), lambda b, h, qi: (b, h, 0, 0, 0))],
        out_specs=pl.BlockSpec((tq, LANES), lambda b, h, qi: (b * nq + qi, h)),
        out_shape=jax.ShapeDtypeStruct((bsz * seq, FOX_HEADS * FOX_HEAD_DIM), BF16),
        scratch_shapes=[pltpu.VMEM((2, 1, tq), F32), pltpu.VMEM((2, 1, tq), F32),
                        pltpu.VMEM((2, LANES, tq), F32), pltpu.VMEM((2, tk, tq), F32)],
        compiler_params=_cparams("parallel", "parallel", "arbitrary"),
        name="attn",
    )(qt, k_aug, vt)


def _swiglu(xn, w1_ref, w3_ref, w2_ref, h_sc, fc):
    f = w1_ref.shape[1]
    for c0 in range(0, f, fc):
        a = _dot(xn, w1_ref[:, c0:c0 + fc])
        b = _dot(xn, w3_ref[:, c0:c0 + fc])
        h_sc[:, c0:c0 + fc] = (a * _sigmoid(a) * b).astype(BF16)
    return _dot(h_sc[...], w2_ref[...])


def _ffn_kernel(h_ref, ya_ref, yb_ref, wo_ref, g_ref, w1_ref, w3_ref, w2_ref, o_ref, h_sc):
    half = ya_ref.shape[1]
    h1 = h_ref[...] + _dot(ya_ref[...], wo_ref[:half, :]) + _dot(yb_ref[...], wo_ref[half:, :])
    xn = _rmsnorm(h1, g_ref[...]).astype(BF16)
    o_ref[...] = h1 + _swiglu(xn, w1_ref, w3_ref, w2_ref, h_sc, FC_FFN)


def _ffn(h, ya, yb, wo, g, w1, w3, w2):
    t, d = h.shape
    tm = TM_FFN
    f = w1.shape[1]
    half = ya.shape[1]
    row = lambda i: (i, 0)
    c2 = lambda i: (0, 0)
    return pl.pallas_call(
        _ffn_kernel,
        grid=(t // tm,),
        in_specs=[pl.BlockSpec((tm, d), row), pl.BlockSpec((tm, half), row), pl.BlockSpec((tm, half), row),
                  pl.BlockSpec((d, d), c2), pl.BlockSpec((1, d), c2),
                  pl.BlockSpec((d, f), c2), pl.BlockSpec((d, f), c2), pl.BlockSpec((f, d), c2)],
        out_specs=pl.BlockSpec((tm, d), row),
        out_shape=jax.ShapeDtypeStruct((t, d), F32),
        scratch_shapes=[pltpu.VMEM((tm, f), BF16)],
        compiler_params=_cparams("parallel"),
        name="ffn",
    )(h, ya, yb, wo, g, w1, w3, w2)


def _gla_kernel(q_ref, k_ref, v_ref, g_ref, lr_ref, gw_ref, gb_ref, hn_ref, o_ref, state_t, *, rows):
    @pl.when(pl.program_id(2) == 0)
    def _():
        state_t[...] = jnp.zeros_like(state_t)

    cs = GLA_CHUNK
    log_a = _log_sigmoid(_dot(lr_ref[...].astype(BF16), gw_ref[...]) + gb_ref[...]) * (1.0 / GLA_TAU)
    rowi = lax.broadcasted_iota(jnp.int32, (cs, GLA_DK), 0)
    tri = (lax.broadcasted_iota(jnp.int32, (cs, cs), 0) >= lax.broadcasted_iota(jnp.int32, (cs, cs), 1))
    for ci in range(rows // cs):
        sl = slice(ci * cs, (ci + 1) * cs)
        b = log_a[sl]
        d = 1
        while d < cs:
            b = b + jnp.where(rowi >= d, pltpu.roll(b, d, 0), 0.0)
            d *= 2
        b_last = b[cs - 1:cs, :]
        q = q_ref[sl, :] * (GLA_DK ** -0.5)
        k = k_ref[sl, :]
        v = v_ref[sl, :]
        q_dec = (q * jnp.exp(b)).astype(BF16)
        k_intra = (k * jnp.exp(-b)).astype(BF16)
        k_state = (k * jnp.exp(b_last - b)).astype(BF16)
        scores = jnp.where(tri, _dot_nt(q_dec, k_intra), 0.0)
        st = state_t[...]
        o = _dot(scores.astype(BF16), v) + _dot_nt(q_dec, st.astype(BF16))
        state_t[...] = st * jnp.exp(b_last) + _dot_tn(v, k_state)
        o = o * lax.rsqrt(jnp.mean(o * o, axis=-1, keepdims=True) + EPS) * hn_ref[...]
        g = g_ref[sl, :]
        o_ref[sl, :] = (o * (g * _sigmoid(g))).astype(o_ref.dtype)


def _gla(qk, v, g, lr, gw, gb, hn, bsz, seq):
    t = qk.shape[0]
    rows = min(R_GLA, seq)
    nr = seq // rows
    nh = GLA_HEADS
    return pl.pallas_call(
        functools.partial(_gla_kernel, rows=rows),
        grid=(bsz, nh, nr),
        in_specs=[pl.BlockSpec((rows, GLA_DK), lambda b, h, r: (b * nr + r, h)),
                  pl.BlockSpec((rows, GLA_DK), lambda b, h, r: (b * nr + r, nh + h)),
                  pl.BlockSpec((rows, GLA_DV), lambda b, h, r: (b * nr + r, h)),
                  pl.BlockSpec((rows, GLA_DV), lambda b, h, r: (b * nr + r, h)),
                  pl.BlockSpec((rows, LANES), lambda b, h, r: (b * nr + r, 0)),
                  pl.BlockSpec((LANES, GLA_DK), lambda b, h, r: (0, h)),
                  pl.BlockSpec((1, GLA_DK), lambda b, h, r: (0, h)),
                  pl.BlockSpec((1, GLA_DV), lambda b, h, r: (0, h))],
        out_specs=pl.BlockSpec((rows, GLA_DV), lambda b, h, r: (b * nr + r, h)),
        out_shape=jax.ShapeDtypeStruct((t, nh * GLA_DV), BF16),
        scratch_shapes=[pltpu.VMEM((GLA_DV, GLA_DK), F32)],
        compiler_params=_cparams("parallel", "parallel", "arbitrary"),
        name="gla",
    )(qk, qk, v, g, lr, gw, gb, hn)


def _router_kernel(h_ref, o_ref, wo_ref, g_ref, rw_ref, h1_ref, xn_ref, info_ref):
    h1 = h_ref[...] + _dot(o_ref[...], wo_ref[...])
    h1_ref[...] = h1
    xn = _rmsnorm(h1, g_ref[...])
    _split_lanes(xn_ref, _pack_bf16_pairs(xn))
    rw = rw_ref[...]
    xh = xn.astype(BF16)
    xl = (xn - xh.astype(F32)).astype(BF16)
    rh = rw.astype(BF16)
    rl = (rw - rh.astype(F32)).astype(BF16)
    logits = _dot(xh, rh) + (_dot(xh, rl) + _dot(xl, rh))
    lane = lax.broadcasted_iota(jnp.int32, logits.shape, 1)
    lg = jnp.where(lane < N_EXPERTS, logits, -jnp.inf)
    m1 = jnp.max(lg, axis=-1, keepdims=True)
    i1 = jnp.min(jnp.where(lg == m1, lane, LANES), axis=-1, keepdims=True)
    lg2 = jnp.where(lane == i1, -jnp.inf, lg)
    m2 = jnp.max(lg2, axis=-1, keepdims=True)
    i2 = jnp.min(jnp.where(lg2 == m2, lane, LANES), axis=-1, keepdims=True)
    e = jnp.exp(m2 - m1)
    w1 = 1.0 / (1.0 + e)
    w2 = e * w1
    info = jnp.where(lane == 0, i1.astype(F32),
                     jnp.where(lane == 1, i2.astype(F32),
                               jnp.where(lane == 2, w1, jnp.where(lane == 3, w2, 0.0))))
    info_ref[...] = info


def _router(h, o, wo, g, rw):
    t, d = h.shape
    tm = TM_ROUTER
    row = lambda i: (i, 0)
    c2 = lambda i: (0, 0)
    return pl.pallas_call(
        _router_kernel,
        grid=(t // tm,),
        in_specs=[pl.BlockSpec((tm, d), row), pl.BlockSpec((tm, d), row), pl.BlockSpec((d, d), c2),
                  pl.BlockSpec((1, d), c2), pl.BlockSpec((d, LANES), c2)],
        out_specs=[pl.BlockSpec((tm, d), row), pl.BlockSpec((d // (2 * LANES), tm, LANES), lambda i: (0, i, 0)),
                   pl.BlockSpec((tm, LANES), row)],
        out_shape=[jax.ShapeDtypeStruct((t, d), F32), jax.ShapeDtypeStruct((d // (2 * LANES), t, LANES), jnp.uint32),
                   jax.ShapeDtypeStruct((t, LANES), F32)],
        compiler_params=_cparams("parallel"),
        name="router",
    )(h, o, wo, g, rw)


def _moe_kernel(te_ref, nu_ref, xs_ref, w1_ref, w3_ref, w2_ref, y_ref, h_sc):
    i = pl.program_id(0)

    @pl.when(i < nu_ref[0])
    def _():
        x = _unpack_bf16_pairs(_join_lanes(xs_ref)).astype(BF16)
        _split_lanes(y_ref, _pack_bf16_pairs(_swiglu(x, w1_ref, w3_ref, w2_ref, h_sc, FC_MOE)))

    @pl.when(i >= nu_ref[0])
    def _():
        y_ref[...] = jnp.zeros_like(y_ref)


def _moe(tile_expert, n_used, xs, w1, w3, w2, layer):
    pieces, p, _ = xs.shape
    tm = TM_MOE
    d, f = w1.shape[2:]
    grid_spec = pltpu.PrefetchScalarGridSpec(
        num_scalar_prefetch=2,
        grid=(p // tm,),
        in_specs=[pl.BlockSpec((pieces, tm, LANES), lambda i, te, nu: (0, i, 0)),
                  pl.BlockSpec((None, None, d, f), lambda i, te, nu: (layer, te[i], 0, 0)),
                  pl.BlockSpec((None, None, d, f), lambda i, te, nu: (layer, te[i], 0, 0)),
                  pl.BlockSpec((None, None, f, d), lambda i, te, nu: (layer, te[i], 0, 0))],
        out_specs=pl.BlockSpec((pieces, tm, LANES), lambda i, te, nu: (0, i, 0)),
        scratch_shapes=[pltpu.VMEM((tm, f), BF16)],
    )
    return pl.pallas_call(
        _moe_kernel,
        grid_spec=grid_spec,
        out_shape=jax.ShapeDtypeStruct((pieces, p, LANES), jnp.uint32),
        compiler_params=pltpu.CompilerParams(dimension_semantics=("arbitrary",), vmem_limit_bytes=VMEM_LIMIT_MOE),
        name="moe",
    )(tile_expert, n_used, xs, w1, w3, w2)


def _combine_kernel(h1_ref, y1_ref, y2_ref, info_ref, gf_ref, o_ref, *, final):
    info = info_ref[...]
    y1 = _unpack_bf16_pairs(_join_lanes(y1_ref))
    y2 = _unpack_bf16_pairs(_join_lanes(y2_ref))
    out = h1_ref[...] + info[:, 2:3] * y1 + info[:, 3:4] * y2
    if final:
        out = _rmsnorm(out, gf_ref[...])
    o_ref[...] = out


def _combine(h1, yg, info, gf, final):
    t, d = h1.shape
    tm = TM_COMBINE
    nt = t // tm
    return pl.pallas_call(
        functools.partial(_combine_kernel, final=final),
        grid=(nt,),
        in_specs=[pl.BlockSpec((tm, d), lambda i: (i, 0)),
                  pl.BlockSpec((d // (2 * LANES), tm, LANES), lambda i: (0, i, 0)),
                  pl.BlockSpec((d // (2 * LANES), tm, LANES), lambda i: (0, nt + i, 0)),
                  pl.BlockSpec((tm, LANES), lambda i: (i, 0)),
                  pl.BlockSpec((1, d), lambda i: (0, 0))],
        out_specs=pl.BlockSpec((tm, d), lambda i: (i, 0)),
        out_shape=jax.ShapeDtypeStruct((t, d), F32),
        compiler_params=_cparams("parallel"),
        name="combine",
    )(h1, yg, yg, info, gf)


def _sc_gather(x, idx):
    pieces, rows, lanes = x.shape
    idx = (jnp.arange(pieces, dtype=jnp.int32)[:, None] * rows + idx[None, :]).reshape(-1)
    out = _sc_gather_pieces(x.reshape(pieces * rows, lanes), idx)
    return out.reshape(pieces, -1, lanes)


def _bf16_bits(a):
    bits = lax.bitcast_convert_type(a, jnp.uint32)
    return (bits + (jnp.uint32(0x7FFF) + ((bits >> 16) & jnp.uint32(1)))) >> 16


def _pack_bf16_pairs(x):
    w = x.shape[1] // 2
    return _bf16_bits(x[:, :w]) | (_bf16_bits(x[:, w:]) << 16)


def _unpack_bf16_pairs(p):
    lo = lax.bitcast_convert_type(p << 16, F32)
    hi = lax.bitcast_convert_type(p & jnp.uint32(0xFFFF0000), F32)
    return jnp.concatenate([lo, hi], axis=1)


def _split_lanes(ref, x):
    for k in range(ref.shape[0]):
        ref[k] = x[:, k * LANES:(k + 1) * LANES].astype(ref.dtype)


def _join_lanes(ref):
    return jnp.concatenate([ref[k] for k in range(ref.shape[0])], axis=1)


def _sc_gather_pieces(x, idx):
    n = idx.shape[0]
    d = x.shape[1]
    win = SC_GATHER_WINDOW
    mesh = plsc.VectorSubcoreMesh(core_axis_name="c", subcore_axis_name="s")

    @functools.partial(pl.kernel, out_type=jax.ShapeDtypeStruct((n, d), x.dtype), mesh=mesh)
    def gather_kernel(x_hbm, i_hbm, o_hbm):
        def body(i_vmem, o_vmem):
            pltpu.sync_copy(x_hbm.at[i_vmem.at[0]], o_vmem)

        pltpu.emit_pipeline(
            body,
            grid=(n // win,),
            in_specs=[pl.BlockSpec((1, win), lambda i: (0, i))],
            out_specs=[pl.BlockSpec((win, d), lambda i: (i, 0))],
            core_axis_name=("c", "s"),
            dimension_semantics=(pltpu.PARALLEL,),
        )(i_hbm, o_hbm)

    return gather_kernel(x, idx.reshape(1, n))


def _routing_plan(info, t):
    tm = TM_MOE
    n_tiles = (2 * t) // tm + N_EXPERTS
    top = jnp.concatenate([info[:, 0], info[:, 1]]).astype(jnp.int32)
    onehot = (top[:, None] == jnp.arange(N_EXPERTS, dtype=jnp.int32)[None, :]).astype(jnp.int32)
    csum = jnp.cumsum(onehot, axis=0)
    rank = jnp.sum((csum - onehot) * onehot, axis=1)
    counts = csum[-1]
    ntile = (counts + tm - 1) // tm
    tile_end = jnp.cumsum(ntile)
    tile_start = tile_end - ntile
    pos = tile_start[top] * tm + rank
    n_used = tile_end[-1]
    tile_ids = jnp.arange(n_tiles, dtype=jnp.int32)
    te = jnp.sum((tile_ids[:, None] >= tile_end[None, :]).astype(jnp.int32), axis=1)
    last = jnp.sum((n_used - 1 >= tile_end).astype(jnp.int32))
    te = jnp.where(tile_ids < n_used, te, last).astype(jnp.int32)
    tok = jnp.arange(t, dtype=jnp.int32)
    token_of = jnp.zeros((n_tiles * tm,), jnp.int32).at[pos].set(jnp.concatenate([tok, tok]))
    return te, n_used.reshape(1).astype(jnp.int32), token_of, pos


def _pad_cols(w, n):
    return jnp.pad(w, ((0, 0), (0, n - w.shape[1])))


def _block_diag(w):
    nb, bs, _ = w.shape
    eye = jnp.eye(nb, dtype=w.dtype)
    return jnp.einsum('nij,nm->nimj', w, eye).reshape(nb * bs, nb * bs)


def kernel(x, norm_mix, norm_ffn, norm_final, ev_w_in, ev_conv_w, ev_conv_b, ev_ga_w, ev_ga_b, ev_gx_w, ev_gx_b, ev_lambda, ev_f_b, ev_w_out, ev_ffn_w1, ev_ffn_w3, ev_ffn_w2, od_w_in, od_gate_w2, od_gate_b, od_head_norm, od_w_out, od_router, od_exp_w1, od_exp_w3, od_exp_w2):
    bsz, seq, d = x.shape
    t = bsz * seq
    depth = norm_mix.shape[0]
    h = x.reshape(t, d)
    n_even_main = 2 * LRU_WIDTH + 3 * FOX_HEADS * FOX_HEAD_DIM
    n_odd_main = 2 * GLA_HEADS * GLA_DK + 2 * GLA_HEADS * GLA_DV
    row = lambda v: v.reshape(1, -1)
    exp_w1, exp_w3, exp_w2 = od_exp_w1.astype(BF16), od_exp_w3.astype(BF16), od_exp_w2.astype(BF16)

    for layer in range(depth):
        j = layer // 2
        g_mix = row(norm_mix[layer])
        g_ffn = row(norm_ffn[layer])
        if layer % 2 == 0:
            w_in = ev_w_in[j]
            w = jnp.concatenate([w_in[:, :n_even_main], _pad_cols(w_in[:, n_even_main:], LANES)], axis=1).astype(BF16)
            xy, qkv, f = _proj(h, g_mix, w, [(0, 2 * LRU_WIDTH, F32), (2 * LRU_WIDTH, 1536, BF16), (n_even_main, LANES, F32)])
            fb = _pad_cols(row(ev_f_b[j]), LANES)
            ya, c = _lru(xy, f, ev_conv_w[j], row(ev_conv_b[j]),
                         _block_diag(ev_ga_w[j]).astype(BF16), _block_diag(ev_gx_w[j]).astype(BF16),
                         row(ev_ga_b[j]), row(ev_gx_b[j]), row(ev_lambda[j]), fb, bsz, seq)
            yb = _attn(*_attn_prep(qkv, c, bsz, seq), bsz, seq)
            h = _ffn(h, ya, yb, ev_w_out[j].astype(BF16), g_ffn,
                     ev_ffn_w1[j].astype(BF16), ev_ffn_w3[j].astype(BF16), ev_ffn_w2[j].astype(BF16))
        else:
            w_in = od_w_in[j]
            w = jnp.concatenate([w_in[:, :n_odd_main], _pad_cols(w_in[:, n_odd_main:], LANES)], axis=1).astype(BF16)
            qk, v, g, lr = _proj(h, g_mix, w, [(0, 1024, F32), (1024, 1024, BF16), (2048, 1024, F32), (n_odd_main, LANES, F32)])
            gw = jnp.pad(od_gate_w2[j], ((0, LANES - GLA_RANK), (0, 0))).astype(BF16)
            o = _gla(qk, v, g, lr, gw, row(od_gate_b[j]), row(od_head_norm[j]), bsz, seq)
            rw = _pad_cols(od_router[j], LANES)
            h1, xn, info = _router(h, o, od_w_out[j].astype(BF16), g_ffn, rw)
            te, n_used, token_of, pos = _routing_plan(info, t)
            xs = _sc_gather(xn, token_of)
            y = _moe(te, n_used, xs, exp_w1, exp_w3, exp_w2, j)
            yg = _sc_gather(y, pos)
            final = layer == depth - 1
            h = _combine(h1, yg, info, row(norm_final), final)
    if depth % 2 == 1:
        raise NotImplementedError("final norm is fused into the last odd layer")
    return h.reshape(bsz, seq, d)
```

```python
import functools

import jax
import jax.numpy as jnp
import numpy as np
from jax import lax
from jax.experimental import pallas as pl
from jax.experimental.pallas import tpu as pltpu
from jax.experimental.pallas import tpu_sc as plsc

F32 = jnp.float32
BF16 = jnp.bfloat16

EPS = 1e-6
D_MODEL = 1024
LRU_WIDTH = 512
LRU_BLOCKS = 8
CONV_W = 4
LRU_C = 8.0
FOX_HEADS = 8
FOX_HEAD_DIM = 64
GLA_HEADS = 4
GLA_DK = 128
GLA_DV = 256
GLA_RANK = 16
GLA_TAU = 16.0
GLA_CHUNK = 64
N_EXPERTS = 8
LANES = 128
SUBLANES = 8
LOG2E = 1.4426950408889634
NEG_BIG = -1e30
VMEM_LIMIT = 56 * 1024 * 1024
VMEM_LIMIT_MOE = 60 * 1024 * 1024

TM_PROJ = 512
TC_LRU = 256
TQ_ATTN = 512
TM_FFN = 512
FC_FFN = 256
R_GLA = 1024
TM_ROUTER = 512
TM_MOE = 256
FC_MOE = 512
TM_COMBINE = 512
SC_GATHER_WINDOW = 128
PIECE = 256


def _cparams(*sem):
    return pltpu.CompilerParams(dimension_semantics=sem, vmem_limit_bytes=VMEM_LIMIT)


def _rmsnorm(x, g):
    return x * lax.rsqrt(jnp.mean(x * x, axis=-1, keepdims=True) + EPS) * g


def _sigmoid(x):
    return 1.0 / (1.0 + jnp.exp(-x))


def _log_sigmoid(z):
    return jnp.minimum(z, 0.0) - jnp.log1p(jnp.exp(-jnp.abs(z)))


def _dot(a, b):
    return jnp.dot(a, b, preferred_element_type=F32)


def _dot_nt(a, b):
    return lax.dot_general(a, b, (((1,), (1,)), ((), ())), preferred_element_type=F32)


def _dot_tn(a, b):
    return lax.dot_general(a, b, (((0,), (0,)), ((), ())), preferred_element_type=F32)


def _proj_kernel(h_ref, g_ref, w_ref, *out_refs, splits):
    xn = _rmsnorm(h_ref[...], g_ref[...]).astype(BF16)
    for o_ref, (start, width) in zip(out_refs, splits):
        o_ref[...] = _dot(xn, w_ref[:, start:start + width]).astype(o_ref.dtype)


def _proj(h, g, w, outs):
    t, d = h.shape
    n = w.shape[1]
    tm = TM_PROJ
    splits = tuple((s, wd) for s, wd, _ in outs)
    return pl.pallas_call(
        functools.partial(_proj_kernel, splits=splits),
        grid=(t // tm,),
        in_specs=[pl.BlockSpec((tm, d), lambda i: (i, 0)),
                  pl.BlockSpec((1, d), lambda i: (0, 0)),
                  pl.BlockSpec((d, n), lambda i: (0, 0))],
        out_specs=[pl.BlockSpec((tm, wd), lambda i: (i, 0)) for _, wd, _ in outs],
        out_shape=[jax.ShapeDtypeStruct((t, wd), dt) for _, wd, dt in outs],
        compiler_params=_cparams("parallel"),
        name="proj",
    )(h, g, w)


def _scan8(a, u):
    row = lax.broadcasted_iota(jnp.int32, u.shape, 0) & (SUBLANES - 1)
    for d in (1, 2, 4):
        m = row >= d
        u_sh = pltpu.roll(u, d, 0)
        if a is None:
            u = jnp.where(m, u + u_sh, u)
        else:
            a_sh = pltpu.roll(a, d, 0)
            u = jnp.where(m, a * u_sh + u, u)
            a = jnp.where(m, a * a_sh, a)
    return a, u


def _lru_kernel(xr_ref, yr_ref, f_ref, cw_ref, cb_ref, wga_ref, wgx_ref, gab_ref, gxb_ref,
                lam_ref, fb_ref, ya_ref, c_ref, xprev, hprev, cprev, hs, *, tc):
    @pl.when(pl.program_id(1) == 0)
    def _():
        xprev[...] = jnp.zeros_like(xprev)
        hprev[...] = jnp.zeros_like(hprev)
        cprev[...] = jnp.zeros_like(cprev)

    x = xr_ref[...]
    p8 = xprev[...]
    row8 = lax.broadcasted_iota(jnp.int32, p8.shape, 0)
    xc = cb_ref[...] + cw_ref[CONV_W - 1:CONV_W, :] * x
    for j in range(CONV_W - 1):
        sh = CONV_W - 1 - j
        xs = pltpu.roll(x, sh, 0)
        head = jnp.where(row8 < sh, pltpu.roll(p8, sh, 0), xs[:SUBLANES])
        xs = jnp.concatenate([head, xs[SUBLANES:]], axis=0)
        xc = xc + cw_ref[j:j + 1, :] * xs
    xprev[...] = x[tc - SUBLANES:, :]

    xb = xc.astype(BF16)
    r = _sigmoid(_dot(xb, wga_ref[...]) + gab_ref[...])
    i = _sigmoid(_dot(xb, wgx_ref[...]) + gxb_ref[...])
    nl = -lam_ref[...]
    softplus_nl = jnp.maximum(nl, 0.0) + jnp.log1p(jnp.exp(-jnp.abs(nl)))
    log_a = (-LRU_C) * r * softplus_nl
    a = jnp.exp(log_a)
    u = jnp.sqrt(-jnp.tanh(log_a) * (1.0 + a * a)) * (i * xc)

    a8, u8 = _scan8(a, u)
    _, lf8 = _scan8(None, _log_sigmoid(f_ref[...] + fb_ref[...]))

    h = hprev[...]
    c = cprev[...]
    for gi in range(tc // SUBLANES):
        rows = slice(gi * SUBLANES, (gi + 1) * SUBLANES)
        blk = a8[rows] * h + u8[rows]
        hs[rows, :] = blk
        h = blk[SUBLANES - 1:SUBLANES, :]
        cblk = lf8[rows] + c
        c_ref[rows, :] = cblk
        c = cblk[SUBLANES - 1:SUBLANES, :]
    hprev[...] = h
    cprev[...] = c

    yr = yr_ref[...]
    gelu = 0.5 * yr * (1.0 + jnp.tanh(0.7978845608028654 * (yr + 0.044715 * (yr * yr * yr))))
    ya_ref[...] = (hs[...] * gelu).astype(BF16)


def _lru(xy, f, cw, cb, wga, wgx, gab, gxb, lam, fb, bsz, seq):
    t = xy.shape[0]
    w = LRU_WIDTH
    tc = TC_LRU
    ns = seq // tc
    row = lambda b, s: (b * ns + s, 0)
    const = lambda b, s: (0, 0)
    return pl.pallas_call(
        functools.partial(_lru_kernel, tc=tc),
        grid=(bsz, ns),
        in_specs=[pl.BlockSpec((tc, w), row),
                  pl.BlockSpec((tc, w), lambda b, s: (b * ns + s, 1)),
                  pl.BlockSpec((tc, LANES), row),
                  pl.BlockSpec((CONV_W, w), const),
                  pl.BlockSpec((1, w), const),
                  pl.BlockSpec((w, w), const),
                  pl.BlockSpec((w, w), const),
                  pl.BlockSpec((1, w), const),
                  pl.BlockSpec((1, w), const),
                  pl.BlockSpec((1, w), const),
                  pl.BlockSpec((1, LANES), const)],
        out_specs=[pl.BlockSpec((tc, w), row), pl.BlockSpec((tc, LANES), row)],
        out_shape=[jax.ShapeDtypeStruct((t, w), BF16), jax.ShapeDtypeStruct((t, LANES), F32)],
        scratch_shapes=[pltpu.VMEM((SUBLANES, w), F32), pltpu.VMEM((1, w), F32),
                        pltpu.VMEM((1, LANES), F32), pltpu.VMEM((tc, w), F32)],
        compiler_params=_cparams("parallel", "arbitrary"),
        name="lru",
    )(xy, xy, f, cw, cb, wga, wgx, gab, gxb, lam, fb)


def _placement_matrices():
    nh, hd = FOX_HEADS, FOX_HEAD_DIM
    w = nh * hd
    pq = np.zeros((nh * LANES, w + 3 * LANES), np.float32)
    pk = np.zeros((w + 3 * LANES, nh * LANES), np.float32)
    for h in range(nh):
        for j in range(hd):
            pq[h * LANES + j, h * hd + j] = 1.0
            pk[h * hd + j, h * LANES + j] = 1.0
        for part in range(3):
            pq[h * LANES + hd + part, w + part * LANES + h] = 1.0
            pk[w + part * LANES + h, h * LANES + hd + 3 + part] = -1.0
    return jnp.asarray(pq, BF16), jnp.asarray(pk, BF16), jnp.eye(w, dtype=BF16)


def _attn_prep_kernel(qkv_ref, c_ref, pq_ref, pk_ref, eye_ref, qt_ref, k_ref, vt_ref):
    hd = FOX_HEAD_DIM
    w = FOX_HEADS * hd
    c = c_ref[...] * LOG2E
    hi = c.astype(BF16)
    r1 = c - hi.astype(F32)
    mid = r1.astype(BF16)
    lo = (r1 - mid.astype(F32)).astype(BF16)
    cs = jnp.concatenate([hi, mid, lo], axis=1)
    qkv = qkv_ref[...]
    q = (qkv[:, :w].astype(F32) * (hd ** -0.5 * LOG2E)).astype(BF16)
    xq = jnp.concatenate([q, cs], axis=1)
    xk = jnp.concatenate([qkv[:, w:2 * w], cs], axis=1)
    qt = _dot_nt(pq_ref[...], xq)
    feat = lax.broadcasted_iota(jnp.int32, (qt.shape[0], 1), 0) & (LANES - 1)
    qt = qt + jnp.where((feat >= hd + 3) & (feat < hd + 6), 1.0, 0.0)
    for h in range(FOX_HEADS):
        qt_ref[h] = qt[h * LANES:(h + 1) * LANES, :].astype(BF16)
    ka = _dot(xk, pk_ref[...])
    col = lax.broadcasted_iota(jnp.int32, (1, ka.shape[1]), 1) & (LANES - 1)
    k_ref[...] = (ka + jnp.where((col >= hd) & (col < hd + 3), 1.0, 0.0)).astype(BF16)
    vt = _dot_nt(eye_ref[...], qkv[:, 2 * w:])
    for hp in range(FOX_HEADS // 2):
        vt_ref[hp, 0] = vt[hp * LANES:(hp + 1) * LANES, :].astype(BF16)


def _attn_prep(qkv, c, bsz, seq):
    nh = FOX_HEADS
    tb = min(TQ_ATTN, seq)
    ns = seq // tb
    pq, pk, eye = _placement_matrices()
    const = lambda b, s: (0, 0)
    return pl.pallas_call(
        _attn_prep_kernel,
        grid=(bsz, ns),
        in_specs=[pl.BlockSpec((tb, qkv.shape[1]), lambda b, s: (b * ns + s, 0)),
                  pl.BlockSpec((tb, LANES), lambda b, s: (b * ns + s, 0)),
                  pl.BlockSpec(pq.shape, const), pl.BlockSpec(pk.shape, const), pl.BlockSpec(eye.shape, const)],
        out_specs=[pl.BlockSpec((None, nh, LANES, tb), lambda b, s: (b, 0, 0, s)),
                   pl.BlockSpec((tb, nh * LANES), lambda b, s: (b * ns + s, 0)),
                   pl.BlockSpec((None, nh // 2, 1, LANES, tb), lambda b, s: (b, 0, s, 0, 0))],
        out_shape=[jax.ShapeDtypeStruct((bsz, nh, LANES, seq), BF16),
                   jax.ShapeDtypeStruct((bsz * seq, nh * LANES), BF16),
                   jax.ShapeDtypeStruct((bsz, nh // 2, ns, LANES, tb), BF16)],
        compiler_params=_cparams("parallel", "parallel"),
        name="attn_prep",
    )(qkv, c, pq, pk, eye)


def _attn_kernel(qt_ref, k_ref, vt_ref, wa_ref, wb_ref, wc_ref, o_ref, wa_out, wb_out, wc_out,
                 m_sc, l_sc, acc_sc, s_sc, *, tk):
    wa_out[...] = wa_ref[...].astype(BF16)
    wb_out[...] = wb_ref[...].astype(BF16)
    wc_out[...] = wc_ref[...].astype(BF16)
    qi = pl.program_id(2)
    m_sc[...] = jnp.full_like(m_sc, NEG_BIG)
    l_sc[...] = jnp.zeros_like(l_sc)
    acc_sc[...] = jnp.zeros_like(acc_sc)

    def logits(kb, hh):
        k2 = k_ref[pl.ds(pl.multiple_of(kb * tk, tk), tk), hh * LANES:(hh + 1) * LANES]
        return _dot(k2, qt_ref[hh])

    def update(kb, hh, s, masked):
        if masked:
            key = lax.broadcasted_iota(jnp.int32, s.shape, 0)
            qry = lax.broadcasted_iota(jnp.int32, s.shape, 1)
            s = jnp.where(key <= qry, s, NEG_BIG)
        m_prev = m_sc[hh]
        m_new = jnp.maximum(m_prev, jnp.max(s, axis=0, keepdims=True))
        alpha = jnp.exp2(m_prev - m_new)
        p = jnp.exp2(s - m_new)
        l_sc[hh] = alpha * l_sc[hh] + jnp.sum(p, axis=0, keepdims=True)
        acc_sc[hh] = alpha * acc_sc[hh] + _dot(vt_ref[kb], p.astype(BF16))
        m_sc[hh] = m_new

    for hh in range(2):
        s_sc[hh] = logits(0, hh)

    def body(kb, carry):
        for hh in range(2):
            s = s_sc[hh]
            s_next = logits(kb + 1, hh)
            update(kb, hh, s, False)
            s_sc[hh] = s_next
        return carry

    lax.fori_loop(0, qi, body, 0)
    for hh in range(2):
        update(qi, hh, s_sc[hh], True)
    feat = lax.broadcasted_iota(jnp.int32, (LANES, 1), 0)
    ot = jnp.where(feat < FOX_HEAD_DIM, acc_sc[0] / l_sc[0], acc_sc[1] / l_sc[1])
    o_ref[...] = ot.T.astype(o_ref.dtype)


def _attn(qt, k_aug, vt, weights, layer, bsz, seq):
    nk, tk = vt.shape[2], vt.shape[4]
    tq = tk
    nq = seq // tq
    npair = FOX_HEADS // 2
    steps = bsz * npair * nq
    flat = [w.reshape(-1, w.shape[-1]) for w in weights]
    rows = [w.shape[0] // (weights[0].shape[0] * steps) for w in flat]
    step = lambda b, h, qi: (b * npair + h) * nq + qi
    in_w = [pl.BlockSpec((r, w.shape[1]), lambda b, h, qi: (layer * steps + step(b, h, qi), 0))
            for r, w in zip(rows, flat)]
    out_w = [pl.BlockSpec((r, w.shape[1]), lambda b, h, qi: (step(b, h, qi), 0)) for r, w in zip(rows, flat)]
    outs = pl.pallas_call(
        functools.partial(_attn_kernel, tk=tk),
        grid=(bsz, npair, nq),
        in_specs=[pl.BlockSpec((None, 2, LANES, tq), lambda b, h, qi: (b, h, 0, qi)),
                  pl.BlockSpec((seq, 2 * LANES), lambda b, h, qi: (b, h)),
                  pl.BlockSpec((None, None, nk, LANES, tk), lambda b, h, qi: (b, h, 0, 0, 0))] + in_w,
        out_specs=[pl.BlockSpec((tq, LANES), lambda b, h, qi: (b * nq + qi, h))] + out_w,
        out_shape=[jax.ShapeDtypeStruct((bsz * seq, FOX_HEADS * FOX_HEAD_DIM), BF16)]
        + [jax.ShapeDtypeStruct((r * steps, w.shape[1]), BF16) for r, w in zip(rows, flat)],
        scratch_shapes=[pltpu.VMEM((2, 1, tq), F32), pltpu.VMEM((2, 1, tq), F32),
                        pltpu.VMEM((2, LANES, tq), F32), pltpu.VMEM((2, tk, tq), F32)],
        compiler_params=_cparams("parallel", "parallel", "arbitrary"),
        name="attn",
    )(qt, k_aug, vt, *flat)
    return outs[0], [o.reshape(w.shape[1:]) for o, w in zip(outs[1:], weights)]


def _swiglu(xn, w1_ref, w3_ref, w2_ref, h_sc, fc):
    f = w1_ref.shape[1]
    for c0 in range(0, f, fc):
        a = _dot(xn, w1_ref[:, c0:c0 + fc])
        b = _dot(xn, w3_ref[:, c0:c0 + fc])
        h_sc[:, c0:c0 + fc] = (a * _sigmoid(a) * b).astype(BF16)
    return _dot(h_sc[...], w2_ref[...])


def _ffn_kernel(h_ref, ya_ref, yb_ref, wo_ref, g_ref, w1_ref, w3_ref, w2_ref, o_ref, h_sc):
    half = ya_ref.shape[1]
    h1 = h_ref[...] + _dot(ya_ref[...], wo_ref[:half, :]) + _dot(yb_ref[...], wo_ref[half:, :])
    xn = _rmsnorm(h1, g_ref[...]).astype(BF16)
    o_ref[...] = h1 + _swiglu(xn, w1_ref, w3_ref, w2_ref, h_sc, FC_FFN)


def _ffn(h, ya, yb, wo, g, w1, w3, w2):
    t, d = h.shape
    tm = TM_FFN
    f = w1.shape[1]
    half = ya.shape[1]
    row = lambda i: (i, 0)
    c2 = lambda i: (0, 0)
    return pl.pallas_call(
        _ffn_kernel,
        grid=(t // tm,),
        in_specs=[pl.BlockSpec((tm, d), row), pl.BlockSpec((tm, half), row), pl.BlockSpec((tm, half), row),
                  pl.BlockSpec((d, d), c2), pl.BlockSpec((1, d), c2),
                  pl.BlockSpec((d, f), c2), pl.BlockSpec((d, f), c2), pl.BlockSpec((f, d), c2)],
        out_specs=pl.BlockSpec((tm, d), row),
        out_shape=jax.ShapeDtypeStruct((t, d), F32),
        scratch_shapes=[pltpu.VMEM((tm, f), BF16)],
        compiler_params=_cparams("parallel"),
        name="ffn",
    )(h, ya, yb, wo, g, w1, w3, w2)


def _gla_kernel(q_ref, k_ref, v_ref, g_ref, lr_ref, gw_ref, gb_ref, hn_ref, o_ref, state_t, *, rows):
    @pl.when(pl.program_id(2) == 0)
    def _():
        state_t[...] = jnp.zeros_like(state_t)

    cs = GLA_CHUNK
    log_a = _log_sigmoid(_dot(lr_ref[...].astype(BF16), gw_ref[...]) + gb_ref[...]) * (1.0 / GLA_TAU)
    rowi = lax.broadcasted_iota(jnp.int32, (cs, GLA_DK), 0)
    tri = (lax.broadcasted_iota(jnp.int32, (cs, cs), 0) >= lax.broadcasted_iota(jnp.int32, (cs, cs), 1))
    for ci in range(rows // cs):
        sl = slice(ci * cs, (ci + 1) * cs)
        b = log_a[sl]
        d = 1
        while d < cs:
            b = b + jnp.where(rowi >= d, pltpu.roll(b, d, 0), 0.0)
            d *= 2
        b_last = b[cs - 1:cs, :]
        q = q_ref[sl, :] * (GLA_DK ** -0.5)
        k = k_ref[sl, :]
        v = v_ref[sl, :]
        q_dec = (q * jnp.exp(b)).astype(BF16)
        k_intra = (k * jnp.exp(-b)).astype(BF16)
        k_state = (k * jnp.exp(b_last - b)).astype(BF16)
        scores = jnp.where(tri, _dot_nt(q_dec, k_intra), 0.0)
        st = state_t[...]
        o = _dot(scores.astype(BF16), v) + _dot_nt(q_dec, st.astype(BF16))
        state_t[...] = st * jnp.exp(b_last) + _dot_tn(v, k_state)
        o = o * lax.rsqrt(jnp.mean(o * o, axis=-1, keepdims=True) + EPS) * hn_ref[...]
        g = g_ref[sl, :]
        o_ref[sl, :] = (o * (g * _sigmoid(g))).astype(o_ref.dtype)


def _gla(qk, v, g, lr, gw, gb, hn, bsz, seq):
    t = qk.shape[0]
    rows = min(R_GLA, seq)
    nr = seq // rows
    nh = GLA_HEADS
    return pl.pallas_call(
        functools.partial(_gla_kernel, rows=rows),
        grid=(bsz, nh, nr),
        in_specs=[pl.BlockSpec((rows, GLA_DK), lambda b, h, r: (b * nr + r, h)),
                  pl.BlockSpec((rows, GLA_DK), lambda b, h, r: (b * nr + r, nh + h)),
                  pl.BlockSpec((rows, GLA_DV), lambda b, h, r: (b * nr + r, h)),
                  pl.BlockSpec((rows, GLA_DV), lambda b, h, r: (b * nr + r, h)),
                  pl.BlockSpec((rows, LANES), lambda b, h, r: (b * nr + r, 0)),
                  pl.BlockSpec((LANES, GLA_DK), lambda b, h, r: (0, h)),
                  pl.BlockSpec((1, GLA_DK), lambda b, h, r: (0, h)),
                  pl.BlockSpec((1, GLA_DV), lambda b, h, r: (0, h))],
        out_specs=pl.BlockSpec((rows, GLA_DV), lambda b, h, r: (b * nr + r, h)),
        out_shape=jax.ShapeDtypeStruct((t, nh * GLA_DV), BF16),
        scratch_shapes=[pltpu.VMEM((GLA_DV, GLA_DK), F32)],
        compiler_params=_cparams("parallel", "parallel", "arbitrary"),
        name="gla",
    )(qk, qk, v, g, lr, gw, gb, hn)


def _router_kernel(h_ref, o_ref, wo_ref, g_ref, rw_ref, h1_ref, xn_ref, info_ref):
    h1 = h_ref[...] + _dot(o_ref[...], wo_ref[...])
    h1_ref[...] = h1
    xn = _rmsnorm(h1, g_ref[...])
    _split_lanes(xn_ref, _pack_bf16_pairs(xn))
    rw = rw_ref[...]
    xh = xn.astype(BF16)
    xl = (xn - xh.astype(F32)).astype(BF16)
    rh = rw.astype(BF16)
    rl = (rw - rh.astype(F32)).astype(BF16)
    logits = _dot(xh, rh) + (_dot(xh, rl) + _dot(xl, rh))
    lane = lax.broadcasted_iota(jnp.int32, logits.shape, 1)
    lg = jnp.where(lane < N_EXPERTS, logits, -jnp.inf)
    m1 = jnp.max(lg, axis=-1, keepdims=True)
    i1 = jnp.min(jnp.where(lg == m1, lane, LANES), axis=-1, keepdims=True)
    lg2 = jnp.where(lane == i1, -jnp.inf, lg)
    m2 = jnp.max(lg2, axis=-1, keepdims=True)
    i2 = jnp.min(jnp.where(lg2 == m2, lane, LANES), axis=-1, keepdims=True)
    e = jnp.exp(m2 - m1)
    w1 = 1.0 / (1.0 + e)
    w2 = e * w1
    info = jnp.where(lane == 0, i1.astype(F32),
                     jnp.where(lane == 1, i2.astype(F32),
                               jnp.where(lane == 2, w1, jnp.where(lane == 3, w2, 0.0))))
    info_ref[...] = info


def _router(h, o, wo, g, rw):
    t, d = h.shape
    tm = TM_ROUTER
    row = lambda i: (i, 0)
    c2 = lambda i: (0, 0)
    return pl.pallas_call(
        _router_kernel,
        grid=(t // tm,),
        in_specs=[pl.BlockSpec((tm, d), row), pl.BlockSpec((tm, d), row), pl.BlockSpec((d, d), c2),
                  pl.BlockSpec((1, d), c2), pl.BlockSpec((d, LANES), c2)],
        out_specs=[pl.BlockSpec((tm, d), row), pl.BlockSpec((d // (2 * PIECE), tm, PIECE), lambda i: (0, i, 0)),
                   pl.BlockSpec((tm, LANES), row)],
        out_shape=[jax.ShapeDtypeStruct((t, d), F32), jax.ShapeDtypeStruct((d // (2 * PIECE), t, PIECE), jnp.uint32),
                   jax.ShapeDtypeStruct((t, LANES), F32)],
        compiler_params=_cparams("parallel"),
        name="router",
    )(h, o, wo, g, rw)


def _moe_kernel(te_ref, nu_ref, xs_ref, w1_ref, w3_ref, w2_ref, y_ref, h_sc):
    i = pl.program_id(0)

    @pl.when(i < nu_ref[0])
    def _():
        x = _unpack_bf16_pairs(_join_lanes(xs_ref)).astype(BF16)
        _split_lanes(y_ref, _pack_bf16_pairs(_swiglu(x, w1_ref, w3_ref, w2_ref, h_sc, FC_MOE)))

    @pl.when(i >= nu_ref[0])
    def _():
        y_ref[...] = jnp.zeros_like(y_ref)


def _moe(tile_expert, n_used, xs, w1, w3, w2):
    pieces, p, width = xs.shape
    tm = TM_MOE
    d, f = w1.shape[1:]
    grid_spec = pltpu.PrefetchScalarGridSpec(
        num_scalar_prefetch=2,
        grid=(p // tm,),
        in_specs=[pl.BlockSpec((pieces, tm, width), lambda i, te, nu: (0, i, 0)),
                  pl.BlockSpec((None, d, f), lambda i, te, nu: (te[i], 0, 0)),
                  pl.BlockSpec((None, d, f), lambda i, te, nu: (te[i], 0, 0)),
                  pl.BlockSpec((None, f, d), lambda i, te, nu: (te[i], 0, 0))],
        out_specs=pl.BlockSpec((pieces, tm, width), lambda i, te, nu: (0, i, 0)),
        scratch_shapes=[pltpu.VMEM((tm, f), BF16)],
    )
    return pl.pallas_call(
        _moe_kernel,
        grid_spec=grid_spec,
        out_shape=jax.ShapeDtypeStruct((pieces, p, width), jnp.uint32),
        compiler_params=pltpu.CompilerParams(dimension_semantics=("arbitrary",), vmem_limit_bytes=VMEM_LIMIT_MOE),
        name="moe",
    )(tile_expert, n_used, xs, w1, w3, w2)


def _combine_kernel(h1_ref, y1_ref, y2_ref, info_ref, gf_ref, o_ref, *, final):
    info = info_ref[...]
    y1 = _unpack_bf16_pairs(_join_lanes(y1_ref))
    y2 = _unpack_bf16_pairs(_join_lanes(y2_ref))
    out = h1_ref[...] + info[:, 2:3] * y1 + info[:, 3:4] * y2
    if final:
        out = _rmsnorm(out, gf_ref[...])
    o_ref[...] = out


def _combine(h1, yg, info, gf, final):
    t, d = h1.shape
    tm = TM_COMBINE
    nt = t // tm
    return pl.pallas_call(
        functools.partial(_combine_kernel, final=final),
        grid=(nt,),
        in_specs=[pl.BlockSpec((tm, d), lambda i: (i, 0)),
                  pl.BlockSpec((d // (2 * PIECE), tm, PIECE), lambda i: (0, i, 0)),
                  pl.BlockSpec((d // (2 * PIECE), tm, PIECE), lambda i: (0, nt + i, 0)),
                  pl.BlockSpec((tm, LANES), lambda i: (i, 0)),
                  pl.BlockSpec((1, d), lambda i: (0, 0))],
        out_specs=pl.BlockSpec((tm, d), lambda i: (i, 0)),
        out_shape=jax.ShapeDtypeStruct((t, d), F32),
        compiler_params=_cparams("parallel"),
        name="combine",
    )(h1, yg, yg, info, gf)


def _sc_gather(x, idx):
    pieces, rows, lanes = x.shape
    idx = (jnp.arange(pieces, dtype=jnp.int32)[:, None] * rows + idx[None, :]).reshape(-1)
    out = _sc_gather_pieces(x.reshape(pieces * rows, lanes), idx)
    return out.reshape(pieces, -1, lanes)


def _bf16_bits(a):
    bits = lax.bitcast_convert_type(a, jnp.uint32)
    return (bits + (jnp.uint32(0x7FFF) + ((bits >> 16) & jnp.uint32(1)))) >> 16


def _pack_bf16_pairs(x):
    w = x.shape[1] // 2
    return _bf16_bits(x[:, :w]) | (_bf16_bits(x[:, w:]) << 16)


def _unpack_bf16_pairs(p):
    lo = lax.bitcast_convert_type(p << 16, F32)
    hi = lax.bitcast_convert_type(p & jnp.uint32(0xFFFF0000), F32)
    return jnp.concatenate([lo, hi], axis=1)


def _split_lanes(ref, x):
    width = ref.shape[2]
    for k in range(ref.shape[0]):
        ref[k] = x[:, k * width:(k + 1) * width].astype(ref.dtype)


def _join_lanes(ref):
    return jnp.concatenate([ref[k] for k in range(ref.shape[0])], axis=1)


def _sc_gather_pieces(x, idx):
    n = idx.shape[0]
    d = x.shape[1]
    win = SC_GATHER_WINDOW
    mesh = plsc.VectorSubcoreMesh(core_axis_name="c", subcore_axis_name="s")

    @functools.partial(pl.kernel, out_type=jax.ShapeDtypeStruct((n, d), x.dtype), mesh=mesh)
    def gather_kernel(x_hbm, i_hbm, o_hbm):
        def body(i_vmem, o_vmem):
            pltpu.sync_copy(x_hbm.at[i_vmem.at[0]], o_vmem)

        pltpu.emit_pipeline(
            body,
            grid=(n // win,),
            in_specs=[pl.BlockSpec((1, win), lambda i: (0, i))],
            out_specs=[pl.BlockSpec((win, d), lambda i: (i, 0))],
            core_axis_name=("c", "s"),
            dimension_semantics=(pltpu.PARALLEL,),
        )(i_hbm, o_hbm)

    return gather_kernel(x, idx.reshape(1, n))


def _routing_plan(info, t):
    tm = TM_MOE
    n_tiles = (2 * t) // tm + N_EXPERTS
    top = jnp.concatenate([info[:, 0], info[:, 1]]).astype(jnp.int32)
    onehot = (top[:, None] == jnp.arange(N_EXPERTS, dtype=jnp.int32)[None, :]).astype(jnp.int32)
    csum = jnp.cumsum(onehot, axis=0)
    rank = jnp.sum((csum - onehot) * onehot, axis=1)
    counts = csum[-1]
    ntile = (counts + tm - 1) // tm
    tile_end = jnp.cumsum(ntile)
    tile_start = tile_end - ntile
    pos = tile_start[top] * tm + rank
    n_used = tile_end[-1]
    tile_ids = jnp.arange(n_tiles, dtype=jnp.int32)
    te = jnp.sum((tile_ids[:, None] >= tile_end[None, :]).astype(jnp.int32), axis=1)
    last = jnp.sum((n_used - 1 >= tile_end).astype(jnp.int32))
    te = jnp.where(tile_ids < n_used, te, last).astype(jnp.int32)
    tok = jnp.arange(t, dtype=jnp.int32)
    token_of = jnp.zeros((n_tiles * tm,), jnp.int32).at[pos].set(jnp.concatenate([tok, tok]))
    return te, n_used.reshape(1).astype(jnp.int32), token_of, pos


def _pad_cols(w, n):
    return jnp.pad(w, ((0, 0), (0, n - w.shape[1])))


def _block_diag(w):
    nb, bs, _ = w.shape
    eye = jnp.eye(nb, dtype=w.dtype)
    return jnp.einsum('nij,nm->nimj', w, eye).reshape(nb * bs, nb * bs)


def kernel(x, norm_mix, norm_ffn, norm_final, ev_w_in, ev_conv_w, ev_conv_b, ev_ga_w, ev_ga_b, ev_gx_w, ev_gx_b, ev_lambda, ev_f_b, ev_w_out, ev_ffn_w1, ev_ffn_w3, ev_ffn_w2, od_w_in, od_gate_w2, od_gate_b, od_head_norm, od_w_out, od_router, od_exp_w1, od_exp_w3, od_exp_w2):
    bsz, seq, d = x.shape
    t = bsz * seq
    depth = norm_mix.shape[0]
    h = x.reshape(t, d)
    n_even_main = 2 * LRU_WIDTH + 3 * FOX_HEADS * FOX_HEAD_DIM
    n_odd_main = 2 * GLA_HEADS * GLA_DK + 2 * GLA_HEADS * GLA_DV
    row = lambda v: v.reshape(1, -1)
    expert_w = None

    for layer in range(depth):
        j = layer // 2
        g_mix = row(norm_mix[layer])
        g_ffn = row(norm_ffn[layer])
        if layer % 2 == 0:
            w_in = ev_w_in[j]
            w = jnp.concatenate([w_in[:, :n_even_main], _pad_cols(w_in[:, n_even_main:], LANES)], axis=1).astype(BF16)
            xy, qkv, f = _proj(h, g_mix, w, [(0, 2 * LRU_WIDTH, F32), (2 * LRU_WIDTH, 1536, BF16), (n_even_main, LANES, F32)])
            fb = _pad_cols(row(ev_f_b[j]), LANES)
            ya, c = _lru(xy, f, ev_conv_w[j], row(ev_conv_b[j]),
                         _block_diag(ev_ga_w[j]).astype(BF16), _block_diag(ev_gx_w[j]).astype(BF16),
                         row(ev_ga_b[j]), row(ev_gx_b[j]), row(ev_lambda[j]), fb, bsz, seq)
            nxt = min(j, od_exp_w1.shape[0] - 1)
            yb, expert_w = _attn(*_attn_prep(qkv, c, bsz, seq), (od_exp_w1, od_exp_w3, od_exp_w2), nxt, bsz, seq)
            h = _ffn(h, ya, yb, ev_w_out[j].astype(BF16), g_ffn,
                     ev_ffn_w1[j].astype(BF16), ev_ffn_w3[j].astype(BF16), ev_ffn_w2[j].astype(BF16))
        else:
            w_in = od_w_in[j]
            w = jnp.concatenate([w_in[:, :n_odd_main], _pad_cols(w_in[:, n_odd_main:], LANES)], axis=1).astype(BF16)
            qk, v, g, lr = _proj(h, g_mix, w, [(0, 1024, F32), (1024, 1024, BF16), (2048, 1024, F32), (n_odd_main, LANES, F32)])
            gw = jnp.pad(od_gate_w2[j], ((0, LANES - GLA_RANK), (0, 0))).astype(BF16)
            o = _gla(qk, v, g, lr, gw, row(od_gate_b[j]), row(od_head_norm[j]), bsz, seq)
            rw = _pad_cols(od_router[j], LANES)
            h1, xn, info = _router(h, o, od_w_out[j].astype(BF16), g_ffn, rw)
            te, n_used, token_of, pos = _routing_plan(info, t)
            xs = _sc_gather(xn, token_of)
            y = _moe(te, n_used, xs, *expert_w)
            yg = _sc_gather(y, pos)
            final = layer == depth - 1
            h = _combine(h1, yg, info, row(norm_final), final)
    if depth % 2 == 1:
        raise NotImplementedError("final norm is fused into the last odd layer")
    return h.reshape(bsz, seq, d)
```

```python
import functools

import jax
import jax.numpy as jnp
import numpy as np
from jax import lax
from jax.experimental import pallas as pl
from jax.experimental.pallas import tpu as pltpu
from jax.experimental.pallas import tpu_sc as plsc

F32 = jnp.float32
BF16 = jnp.bfloat16

EPS = 1e-6
D_MODEL = 1024
LRU_WIDTH = 512
LRU_BLOCKS = 8
CONV_W = 4
LRU_C = 8.0
FOX_HEADS = 8
FOX_HEAD_DIM = 64
GLA_HEADS = 4
GLA_DK = 128
GLA_DV = 256
GLA_RANK = 16
GLA_TAU = 16.0
GLA_CHUNK = 64
N_EXPERTS = 8
LANES = 128
SUBLANES = 8
LOG2E = 1.4426950408889634
NEG_BIG = -1e30
VMEM_LIMIT = 56 * 1024 * 1024
VMEM_LIMIT_MOE = 60 * 1024 * 1024

TM_PROJ = 1024
TC_LRU = 512
TQ_ATTN = 512
TM_FFN = 512
FC_FFN = 256
R_GLA = 1024
TM_ROUTER = 1024
TM_MOE = 256
FC_MOE = 512
TM_COMBINE = 1024
SC_GATHER_WINDOW = 128
PIECE = 256


def _cparams(*sem):
    return pltpu.CompilerParams(dimension_semantics=sem, vmem_limit_bytes=VMEM_LIMIT)


def _rmsnorm(x, g):
    return x * lax.rsqrt(jnp.mean(x * x, axis=-1, keepdims=True) + EPS) * g


def _sigmoid(x):
    return 1.0 / (1.0 + jnp.exp(-x))


def _log_sigmoid(z):
    return jnp.minimum(z, 0.0) - jnp.log1p(jnp.exp(-jnp.abs(z)))


def _dot(a, b):
    return jnp.dot(a, b, preferred_element_type=F32)


def _dot_nt(a, b):
    return lax.dot_general(a, b, (((1,), (1,)), ((), ())), preferred_element_type=F32)


def _dot_tn(a, b):
    return lax.dot_general(a, b, (((0,), (0,)), ((), ())), preferred_element_type=F32)


def _proj_kernel(h_ref, g_ref, w_ref, *out_refs, splits):
    xn = _rmsnorm(h_ref[...], g_ref[...]).astype(BF16)
    for o_ref, (start, width) in zip(out_refs, splits):
        o_ref[...] = _dot(xn, w_ref[:, start:start + width]).astype(o_ref.dtype)


def _proj(h, g, w, outs):
    t, d = h.shape
    n = w.shape[1]
    tm = TM_PROJ
    splits = tuple((s, wd) for s, wd, _ in outs)
    return pl.pallas_call(
        functools.partial(_proj_kernel, splits=splits),
        grid=(t // tm,),
        in_specs=[pl.BlockSpec((tm, d), lambda i: (i, 0)),
                  pl.BlockSpec((1, d), lambda i: (0, 0)),
                  pl.BlockSpec((d, n), lambda i: (0, 0))],
        out_specs=[pl.BlockSpec((tm, wd), lambda i: (i, 0)) for _, wd, _ in outs],
        out_shape=[jax.ShapeDtypeStruct((t, wd), dt) for _, wd, dt in outs],
        compiler_params=_cparams("parallel"),
        name="proj",
    )(h, g, w)


def _scan8(a, u):
    row = lax.broadcasted_iota(jnp.int32, u.shape, 0) & (SUBLANES - 1)
    for d in (1, 2, 4):
        m = row >= d
        u_sh = pltpu.roll(u, d, 0)
        if a is None:
            u = jnp.where(m, u + u_sh, u)
        else:
            a_sh = pltpu.roll(a, d, 0)
            u = jnp.where(m, a * u_sh + u, u)
            a = jnp.where(m, a * a_sh, a)
    return a, u


def _lru_kernel(xr_ref, yr_ref, f_ref, cw_ref, cb_ref, wga_ref, wgx_ref, gab_ref, gxb_ref,
                lam_ref, fb_ref, ya_ref, c_ref, xprev, hprev, cprev, hs, *, tc):
    @pl.when(pl.program_id(1) == 0)
    def _():
        xprev[...] = jnp.zeros_like(xprev)
        hprev[...] = jnp.zeros_like(hprev)
        cprev[...] = jnp.zeros_like(cprev)

    x = xr_ref[...]
    p8 = xprev[...]
    row8 = lax.broadcasted_iota(jnp.int32, p8.shape, 0)
    xc = cb_ref[...] + cw_ref[CONV_W - 1:CONV_W, :] * x
    for j in range(CONV_W - 1):
        sh = CONV_W - 1 - j
        xs = pltpu.roll(x, sh, 0)
        head = jnp.where(row8 < sh, pltpu.roll(p8, sh, 0), xs[:SUBLANES])
        xs = jnp.concatenate([head, xs[SUBLANES:]], axis=0)
        xc = xc + cw_ref[j:j + 1, :] * xs
    xprev[...] = x[tc - SUBLANES:, :]

    xb = xc.astype(BF16)
    r = _sigmoid(_dot(xb, wga_ref[...]) + gab_ref[...])
    i = _sigmoid(_dot(xb, wgx_ref[...]) + gxb_ref[...])
    nl = -lam_ref[...]
    softplus_nl = jnp.maximum(nl, 0.0) + jnp.log1p(jnp.exp(-jnp.abs(nl)))
    log_a = (-LRU_C) * r * softplus_nl
    a = jnp.exp(log_a)
    u = jnp.sqrt(-jnp.tanh(log_a) * (1.0 + a * a)) * (i * xc)

    a8, u8 = _scan8(a, u)
    _, lf8 = _scan8(None, _log_sigmoid(f_ref[...] + fb_ref[...]))

    h = hprev[...]
    c = cprev[...]
    for gi in range(tc // SUBLANES):
        rows = slice(gi * SUBLANES, (gi + 1) * SUBLANES)
        blk = a8[rows] * h + u8[rows]
        hs[rows, :] = blk
        h = blk[SUBLANES - 1:SUBLANES, :]
        cblk = lf8[rows] + c
        c_ref[rows, :] = cblk
        c = cblk[SUBLANES - 1:SUBLANES, :]
    hprev[...] = h
    cprev[...] = c

    yr = yr_ref[...]
    gelu = 0.5 * yr * (1.0 + jnp.tanh(0.7978845608028654 * (yr + 0.044715 * (yr * yr * yr))))
    ya_ref[...] = (hs[...] * gelu).astype(BF16)


def _lru(xy, f, cw, cb, wga, wgx, gab, gxb, lam, fb, bsz, seq):
    t = xy.shape[0]
    w = LRU_WIDTH
    tc = TC_LRU
    ns = seq // tc
    row = lambda b, s: (b * ns + s, 0)
    const = lambda b, s: (0, 0)
    return pl.pallas_call(
        functools.partial(_lru_kernel, tc=tc),
        grid=(bsz, ns),
        in_specs=[pl.BlockSpec((tc, w), row),
                  pl.BlockSpec((tc, w), lambda b, s: (b * ns + s, 1)),
                  pl.BlockSpec((tc, LANES), row),
                  pl.BlockSpec((CONV_W, w), const),
                  pl.BlockSpec((1, w), const),
                  pl.BlockSpec((w, w), const),
                  pl.BlockSpec((w, w), const),
                  pl.BlockSpec((1, w), const),
                  pl.BlockSpec((1, w), const),
                  pl.BlockSpec((1, w), const),
                  pl.BlockSpec((1, LANES), const)],
        out_specs=[pl.BlockSpec((tc, w), row), pl.BlockSpec((tc, LANES), row)],
        out_shape=[jax.ShapeDtypeStruct((t, w), BF16), jax.ShapeDtypeStruct((t, LANES), F32)],
        scratch_shapes=[pltpu.VMEM((SUBLANES, w), F32), pltpu.VMEM((1, w), F32),
                        pltpu.VMEM((1, LANES), F32), pltpu.VMEM((tc, w), F32)],
        compiler_params=_cparams("parallel", "arbitrary"),
        name="lru",
    )(xy, xy, f, cw, cb, wga, wgx, gab, gxb, lam, fb)


def _placement_matrices():
    nh, hd = FOX_HEADS, FOX_HEAD_DIM
    w = nh * hd
    pq = np.zeros((nh * LANES, w + 3 * LANES), np.float32)
    pk = np.zeros((w + 3 * LANES, nh * LANES), np.float32)
    for h in range(nh):
        for j in range(hd):
            pq[h * LANES + j, h * hd + j] = 1.0
            pk[h * hd + j, h * LANES + j] = 1.0
        for part in range(3):
            pq[h * LANES + hd + part, w + part * LANES + h] = 1.0
            pk[w + part * LANES + h, h * LANES + hd + 3 + part] = -1.0
    return jnp.asarray(pq, BF16), jnp.asarray(pk, BF16), jnp.eye(w, dtype=BF16)


def _attn_prep_kernel(qkv_ref, c_ref, pq_ref, pk_ref, eye_ref, qt_ref, k_ref, vt_ref):
    hd = FOX_HEAD_DIM
    w = FOX_HEADS * hd
    c = c_ref[...] * LOG2E
    hi = c.astype(BF16)
    r1 = c - hi.astype(F32)
    mid = r1.astype(BF16)
    lo = (r1 - mid.astype(F32)).astype(BF16)
    cs = jnp.concatenate([hi, mid, lo], axis=1)
    qkv = qkv_ref[...]
    q = (qkv[:, :w].astype(F32) * (hd ** -0.5 * LOG2E)).astype(BF16)
    xq = jnp.concatenate([q, cs], axis=1)
    xk = jnp.concatenate([qkv[:, w:2 * w], cs], axis=1)
    qt = _dot_nt(pq_ref[...], xq)
    feat = lax.broadcasted_iota(jnp.int32, (qt.shape[0], 1), 0) & (LANES - 1)
    qt = qt + jnp.where((feat >= hd + 3) & (feat < hd + 6), 1.0, 0.0)
    for h in range(FOX_HEADS):
        qt_ref[h] = qt[h * LANES:(h + 1) * LANES, :].astype(BF16)
    ka = _dot(xk, pk_ref[...])
    col = lax.broadcasted_iota(jnp.int32, (1, ka.shape[1]), 1) & (LANES - 1)
    k_ref[...] = (ka + jnp.where((col >= hd) & (col < hd + 3), 1.0, 0.0)).astype(BF16)
    vt = _dot_nt(eye_ref[...], qkv[:, 2 * w:])
    for hp in range(FOX_HEADS // 2):
        vt_ref[hp, 0] = vt[hp * LANES:(hp + 1) * LANES, :].astype(BF16)


def _attn_prep(qkv, c, bsz, seq):
    nh = FOX_HEADS
    tb = min(TQ_ATTN, seq)
    ns = seq // tb
    pq, pk, eye = _placement_matrices()
    const = lambda b, s: (0, 0)
    return pl.pallas_call(
        _attn_prep_kernel,
        grid=(bsz, ns),
        in_specs=[pl.BlockSpec((tb, qkv.shape[1]), lambda b, s: (b * ns + s, 0)),
                  pl.BlockSpec((tb, LANES), lambda b, s: (b * ns + s, 0)),
                  pl.BlockSpec(pq.shape, const), pl.BlockSpec(pk.shape, const), pl.BlockSpec(eye.shape, const)],
        out_specs=[pl.BlockSpec((None, nh, LANES, tb), lambda b, s: (b, 0, 0, s)),
                   pl.BlockSpec((tb, nh * LANES), lambda b, s: (b * ns + s, 0)),
                   pl.BlockSpec((None, nh // 2, 1, LANES, tb), lambda b, s: (b, 0, s, 0, 0))],
        out_shape=[jax.ShapeDtypeStruct((bsz, nh, LANES, seq), BF16),
                   jax.ShapeDtypeStruct((bsz * seq, nh * LANES), BF16),
                   jax.ShapeDtypeStruct((bsz, nh // 2, ns, LANES, tb), BF16)],
        compiler_params=_cparams("parallel", "parallel"),
        name="attn_prep",
    )(qkv, c, pq, pk, eye)


def _attn_kernel(qt_ref, k_ref, vt_ref, wa_ref, wb_ref, wc_ref, o_ref, wa_out, wb_out, wc_out,
                 m_sc, l_sc, acc_sc, s_sc, *, tk):
    wa_out[...] = wa_ref[...].astype(BF16)
    wb_out[...] = wb_ref[...].astype(BF16)
    wc_out[...] = wc_ref[...].astype(BF16)
    qi = pl.program_id(2)
    m_sc[...] = jnp.full_like(m_sc, NEG_BIG)
    l_sc[...] = jnp.zeros_like(l_sc)
    acc_sc[...] = jnp.zeros_like(acc_sc)

    def logits(kb, hh):
        k2 = k_ref[pl.ds(pl.multiple_of(kb * tk, tk), tk), hh * LANES:(hh + 1) * LANES]
        return _dot(k2, qt_ref[hh])

    def update(kb, hh, s, masked):
        if masked:
            key = lax.broadcasted_iota(jnp.int32, s.shape, 0)
            qry = lax.broadcasted_iota(jnp.int32, s.shape, 1)
            s = jnp.where(key <= qry, s, NEG_BIG)
        m_prev = m_sc[hh]
        m_new = jnp.maximum(m_prev, jnp.max(s, axis=0, keepdims=True))
        alpha = jnp.exp2(m_prev - m_new)
        p = jnp.exp2(s - m_new)
        l_sc[hh] = alpha * l_sc[hh] + jnp.sum(p, axis=0, keepdims=True)
        acc_sc[hh] = alpha * acc_sc[hh] + _dot(vt_ref[kb], p.astype(BF16))
        m_sc[hh] = m_new

    for hh in range(2):
        s_sc[hh] = logits(0, hh)

    def body(kb, carry):
        for hh in range(2):
            s = s_sc[hh]
            s_next = logits(kb + 1, hh)
            update(kb, hh, s, False)
            s_sc[hh] = s_next
        return carry

    lax.fori_loop(0, qi, body, 0)
    for hh in range(2):
        update(qi, hh, s_sc[hh], True)
    feat = lax.broadcasted_iota(jnp.int32, (LANES, 1), 0)
    ot = jnp.where(feat < FOX_HEAD_DIM, acc_sc[0] / l_sc[0], acc_sc[1] / l_sc[1])
    o_ref[...] = ot.T.astype(o_ref.dtype)


def _attn(qt, k_aug, vt, weights, layer, bsz, seq):
    nk, tk = vt.shape[2], vt.shape[4]
    tq = tk
    nq = seq // tq
    npair = FOX_HEADS // 2
    steps = bsz * npair * nq
    flat = [w.reshape(-1, w.shape[-1]) for w in weights]
    rows = [w.shape[0] // (weights[0].shape[0] * steps) for w in flat]
    step = lambda b, h, qi: (b * npair + h) * nq + qi
    in_w = [pl.BlockSpec((r, w.shape[1]), lambda b, h, qi: (layer * steps + step(b, h, qi), 0))
            for r, w in zip(rows, flat)]
    out_w = [pl.BlockSpec((r, w.shape[1]), lambda b, h, qi: (step(b, h, qi), 0)) for r, w in zip(rows, flat)]
    outs = pl.pallas_call(
        functools.partial(_attn_kernel, tk=tk),
        grid=(bsz, npair, nq),
        in_specs=[pl.BlockSpec((None, 2, LANES, tq), lambda b, h, qi: (b, h, 0, qi)),
                  pl.BlockSpec((seq, 2 * LANES), lambda b, h, qi: (b, h)),
                  pl.BlockSpec((None, None, nk, LANES, tk), lambda b, h, qi: (b, h, 0, 0, 0))] + in_w,
        out_specs=[pl.BlockSpec((tq, LANES), lambda b, h, qi: (b * nq + qi, h))] + out_w,
        out_shape=[jax.ShapeDtypeStruct((bsz * seq, FOX_HEADS * FOX_HEAD_DIM), BF16)]
        + [jax.ShapeDtypeStruct((r * steps, w.shape[1]), BF16) for r, w in zip(rows, flat)],
        scratch_shapes=[pltpu.VMEM((2, 1, tq), F32), pltpu.VMEM((2, 1, tq), F32),
                        pltpu.VMEM((2, LANES, tq), F32), pltpu.VMEM((2, tk, tq), F32)],
        compiler_params=_cparams("parallel", "parallel", "arbitrary"),
        name="attn",
    )(qt, k_aug, vt, *flat)
    return outs[0], [o.reshape(w.shape[1:]) for o, w in zip(outs[1:], weights)]


def _swiglu(xn, w1_ref, w3_ref, w2_ref, h_sc, fc):
    f = w1_ref.shape[1]
    for c0 in range(0, f, fc):
        a = _dot(xn, w1_ref[:, c0:c0 + fc])
        b = _dot(xn, w3_ref[:, c0:c0 + fc])
        h_sc[:, c0:c0 + fc] = (a * _sigmoid(a) * b).astype(BF16)
    return _dot(h_sc[...], w2_ref[...])


def _ffn_kernel(h_ref, ya_ref, yb_ref, wo_ref, g_ref, w1_ref, w3_ref, w2_ref, o_ref, h_sc):
    half = ya_ref.shape[1]
    h1 = h_ref[...] + _dot(ya_ref[...], wo_ref[:half, :]) + _dot(yb_ref[...], wo_ref[half:, :])
    xn = _rmsnorm(h1, g_ref[...]).astype(BF16)
    o_ref[...] = h1 + _swiglu(xn, w1_ref, w3_ref, w2_ref, h_sc, FC_FFN)


def _ffn(h, ya, yb, wo, g, w1, w3, w2):
    t, d = h.shape
    tm = TM_FFN
    f = w1.shape[1]
    half = ya.shape[1]
    row = lambda i: (i, 0)
    c2 = lambda i: (0, 0)
    return pl.pallas_call(
        _ffn_kernel,
        grid=(t // tm,),
        in_specs=[pl.BlockSpec((tm, d), row), pl.BlockSpec((tm, half), row), pl.BlockSpec((tm, half), row),
                  pl.BlockSpec((d, d), c2), pl.BlockSpec((1, d), c2),
                  pl.BlockSpec((d, f), c2), pl.BlockSpec((d, f), c2), pl.BlockSpec((f, d), c2)],
        out_specs=pl.BlockSpec((tm, d), row),
        out_shape=jax.ShapeDtypeStruct((t, d), F32),
        scratch_shapes=[pltpu.VMEM((tm, f), BF16)],
        compiler_params=_cparams("parallel"),
        name="ffn",
    )(h, ya, yb, wo, g, w1, w3, w2)


def _gla_kernel(q_ref, k_ref, v_ref, g_ref, lr_ref, gw_ref, gb_ref, hn_ref, o_ref, state_t, *, rows):
    @pl.when(pl.program_id(2) == 0)
    def _():
        state_t[...] = jnp.zeros_like(state_t)

    cs = GLA_CHUNK
    log_a = _log_sigmoid(_dot(lr_ref[...].astype(BF16), gw_ref[...]) + gb_ref[...]) * (1.0 / GLA_TAU)
    rowi = lax.broadcasted_iota(jnp.int32, (cs, GLA_DK), 0)
    tri = (lax.broadcasted_iota(jnp.int32, (cs, cs), 0) >= lax.broadcasted_iota(jnp.int32, (cs, cs), 1))
    for ci in range(rows // cs):
        sl = slice(ci * cs, (ci + 1) * cs)
        b = log_a[sl]
        d = 1
        while d < cs:
            b = b + jnp.where(rowi >= d, pltpu.roll(b, d, 0), 0.0)
            d *= 2
        b_last = b[cs - 1:cs, :]
        q = q_ref[sl, :] * (GLA_DK ** -0.5)
        k = k_ref[sl, :]
        v = v_ref[sl, :]
        q_dec = (q * jnp.exp(b)).astype(BF16)
        k_intra = (k * jnp.exp(-b)).astype(BF16)
        k_state = (k * jnp.exp(b_last - b)).astype(BF16)
        scores = jnp.where(tri, _dot_nt(q_dec, k_intra), 0.0)
        st = state_t[...]
        o = _dot(scores.astype(BF16), v) + _dot_nt(q_dec, st.astype(BF16))
        state_t[...] = st * jnp.exp(b_last) + _dot_tn(v, k_state)
        o = o * lax.rsqrt(jnp.mean(o * o, axis=-1, keepdims=True) + EPS) * hn_ref[...]
        g = g_ref[sl, :]
        o_ref[sl, :] = (o * (g * _sigmoid(g))).astype(o_ref.dtype)


def _gla(qk, v, g, lr, gw, gb, hn, bsz, seq):
    t = qk.shape[0]
    rows = min(R_GLA, seq)
    nr = seq // rows
    nh = GLA_HEADS
    return pl.pallas_call(
        functools.partial(_gla_kernel, rows=rows),
        grid=(bsz, nh, nr),
        in_specs=[pl.BlockSpec((rows, GLA_DK), lambda b, h, r: (b * nr + r, h)),
                  pl.BlockSpec((rows, GLA_DK), lambda b, h, r: (b * nr + r, nh + h)),
                  pl.BlockSpec((rows, GLA_DV), lambda b, h, r: (b * nr + r, h)),
                  pl.BlockSpec((rows, GLA_DV), lambda b, h, r: (b * nr + r, h)),
                  pl.BlockSpec((rows, LANES), lambda b, h, r: (b * nr + r, 0)),
                  pl.BlockSpec((LANES, GLA_DK), lambda b, h, r: (0, h)),
                  pl.BlockSpec((1, GLA_DK), lambda b, h, r: (0, h)),
                  pl.BlockSpec((1, GLA_DV), lambda b, h, r: (0, h))],
        out_specs=pl.BlockSpec((rows, GLA_DV), lambda b, h, r: (b * nr + r, h)),
        out_shape=jax.ShapeDtypeStruct((t, nh * GLA_DV), BF16),
        scratch_shapes=[pltpu.VMEM((GLA_DV, GLA_DK), F32)],
        compiler_params=_cparams("parallel", "parallel", "arbitrary"),
        name="gla",
    )(qk, qk, v, g, lr, gw, gb, hn)


def _router_kernel(h_ref, o_ref, wo_ref, g_ref, rw_ref, h1_ref, xn_ref, info_ref):
    h1 = h_ref[...] + _dot(o_ref[...], wo_ref[...])
    h1_ref[...] = h1
    xn = _rmsnorm(h1, g_ref[...])
    _split_lanes(xn_ref, _pack_bf16_pairs(xn))
    rw = rw_ref[...]
    xh = xn.astype(BF16)
    xl = (xn - xh.astype(F32)).astype(BF16)
    rh = rw.astype(BF16)
    rl = (rw - rh.astype(F32)).astype(BF16)
    logits = _dot(xh, rh) + (_dot(xh, rl) + _dot(xl, rh))
    lane = lax.broadcasted_iota(jnp.int32, logits.shape, 1)
    lg = jnp.where(lane < N_EXPERTS, logits, -jnp.inf)
    m1 = jnp.max(lg, axis=-1, keepdims=True)
    i1 = jnp.min(jnp.where(lg == m1, lane, LANES), axis=-1, keepdims=True)
    lg2 = jnp.where(lane == i1, -jnp.inf, lg)
    m2 = jnp.max(lg2, axis=-1, keepdims=True)
    i2 = jnp.min(jnp.where(lg2 == m2, lane, LANES), axis=-1, keepdims=True)
    e = jnp.exp(m2 - m1)
    w1 = 1.0 / (1.0 + e)
    w2 = e * w1
    info = jnp.where(lane == 0, i1.astype(F32),
                     jnp.where(lane == 1, i2.astype(F32),
                               jnp.where(lane == 2, w1, jnp.where(lane == 3, w2, 0.0))))
    info_ref[...] = info


def _router(h, o, wo, g, rw):
    t, d = h.shape
    tm = TM_ROUTER
    row = lambda i: (i, 0)
    c2 = lambda i: (0, 0)
    return pl.pallas_call(
        _router_kernel,
        grid=(t // tm,),
        in_specs=[pl.BlockSpec((tm, d), row), pl.BlockSpec((tm, d), row), pl.BlockSpec((d, d), c2),
                  pl.BlockSpec((1, d), c2), pl.BlockSpec((d, LANES), c2)],
        out_specs=[pl.BlockSpec((tm, d), row), pl.BlockSpec((d // (2 * PIECE), tm, PIECE), lambda i: (0, i, 0)),
                   pl.BlockSpec((tm, LANES), row)],
        out_shape=[jax.ShapeDtypeStruct((t, d), F32), jax.ShapeDtypeStruct((d // (2 * PIECE), t, PIECE), jnp.uint32),
                   jax.ShapeDtypeStruct((t, LANES), F32)],
        compiler_params=_cparams("parallel"),
        name="router",
    )(h, o, wo, g, rw)


def _moe_kernel(te_ref, nu_ref, valid_ref, xs_ref, w1_ref, w3_ref, w2_ref, y_ref, h_sc):
    i = pl.program_id(0)

    @pl.when(i < nu_ref[0])
    def _():
        packed = _join_lanes(xs_ref)
        row = lax.broadcasted_iota(jnp.int32, packed.shape, 0)
        packed = jnp.where(row < valid_ref[i], packed, jnp.uint32(0))
        x = _unpack_bf16_pairs(packed).astype(BF16)
        _split_lanes(y_ref, _pack_bf16_pairs(_swiglu(x, w1_ref, w3_ref, w2_ref, h_sc, FC_MOE)))

    @pl.when(i >= nu_ref[0])
    def _():
        y_ref[...] = jnp.zeros_like(y_ref)


def _moe(tile_expert, n_used, valid, xs, w1, w3, w2):
    pieces, p, width = xs.shape
    tm = TM_MOE
    d, f = w1.shape[1:]
    grid_spec = pltpu.PrefetchScalarGridSpec(
        num_scalar_prefetch=3,
        grid=(p // tm,),
        in_specs=[pl.BlockSpec((pieces, tm, width), lambda i, te, nu, va: (0, i, 0)),
                  pl.BlockSpec((None, d, f), lambda i, te, nu, va: (te[i], 0, 0)),
                  pl.BlockSpec((None, d, f), lambda i, te, nu, va: (te[i], 0, 0)),
                  pl.BlockSpec((None, f, d), lambda i, te, nu, va: (te[i], 0, 0))],
        out_specs=pl.BlockSpec((pieces, tm, width), lambda i, te, nu, va: (0, i, 0)),
        scratch_shapes=[pltpu.VMEM((tm, f), BF16)],
    )
    return pl.pallas_call(
        _moe_kernel,
        grid_spec=grid_spec,
        out_shape=jax.ShapeDtypeStruct((pieces, p, width), jnp.uint32),
        compiler_params=pltpu.CompilerParams(dimension_semantics=("arbitrary",), vmem_limit_bytes=VMEM_LIMIT_MOE),
        name="moe",
    )(tile_expert, n_used, valid, xs, w1, w3, w2)


def _combine_kernel(h1_ref, y1_ref, y2_ref, info_ref, gf_ref, o_ref, *, final):
    info = info_ref[...]
    y1 = _unpack_bf16_pairs(_join_lanes(y1_ref))
    y2 = _unpack_bf16_pairs(_join_lanes(y2_ref))
    out = h1_ref[...] + info[:, 2:3] * y1 + info[:, 3:4] * y2
    if final:
        out = _rmsnorm(out, gf_ref[...])
    o_ref[...] = out


def _combine(h1, yg, info, gf, final):
    t, d = h1.shape
    tm = TM_COMBINE
    nt = t // tm
    return pl.pallas_call(
        functools.partial(_combine_kernel, final=final),
        grid=(nt,),
        in_specs=[pl.BlockSpec((tm, d), lambda i: (i, 0)),
                  pl.BlockSpec((d // (2 * PIECE), tm, PIECE), lambda i: (0, i, 0)),
                  pl.BlockSpec((d // (2 * PIECE), tm, PIECE), lambda i: (0, nt + i, 0)),
                  pl.BlockSpec((tm, LANES), lambda i: (i, 0)),
                  pl.BlockSpec((1, d), lambda i: (0, 0))],
        out_specs=pl.BlockSpec((tm, d), lambda i: (i, 0)),
        out_shape=jax.ShapeDtypeStruct((t, d), F32),
        compiler_params=_cparams("parallel"),
        name="combine",
    )(h1, yg, yg, info, gf)


def _sc_gather(x, idx):
    pieces, rows, lanes = x.shape
    idx = (jnp.arange(pieces, dtype=jnp.int32)[:, None] * rows + idx[None, :]).reshape(-1)
    out = _sc_gather_pieces(x.reshape(pieces * rows, lanes), idx)
    return out.reshape(pieces, -1, lanes)


def _sc_scatter(x, pos, out_rows):
    pieces, t, width = x.shape
    idx = (jnp.arange(pieces, dtype=jnp.int32)[:, None] * out_rows + pos[None, :]).reshape(-1)
    out = _sc_scatter_pieces(x.reshape(pieces * t, width), idx, pos.shape[0], pieces * out_rows)
    return out.reshape(pieces, out_rows, width)


def _sc_scatter_pieces(x, idx, slots, out_rows):
    n = idx.shape[0]
    d = x.shape[1]
    win = SC_GATHER_WINDOW
    pieces = n // slots
    nb_slots = slots // win
    nb_rows = x.shape[0] // pieces // win
    mesh = plsc.VectorSubcoreMesh(core_axis_name="c", subcore_axis_name="s")

    @functools.partial(pl.kernel, out_type=jax.ShapeDtypeStruct((out_rows, d), x.dtype), mesh=mesh)
    def scatter_kernel(x_hbm, i_hbm, o_hbm):
        def body(x_vmem, i_vmem):
            pltpu.sync_copy(x_vmem, o_hbm.at[i_vmem.at[0]])

        pltpu.emit_pipeline(
            body,
            grid=(n // win,),
            in_specs=[pl.BlockSpec((win, d), lambda i: ((i // nb_slots) * nb_rows + (i % nb_slots) % nb_rows, 0)),
                      pl.BlockSpec((1, win), lambda i: (0, i))],
            out_specs=[],
            core_axis_name=("c", "s"),
            dimension_semantics=(pltpu.PARALLEL,),
        )(x_hbm, i_hbm)

    return scatter_kernel(x, idx.reshape(1, n))


def _bf16_bits(a):
    bits = lax.bitcast_convert_type(a, jnp.uint32)
    return (bits + (jnp.uint32(0x7FFF) + ((bits >> 16) & jnp.uint32(1)))) >> 16


def _pack_bf16_pairs(x):
    w = x.shape[1] // 2
    return _bf16_bits(x[:, :w]) | (_bf16_bits(x[:, w:]) << 16)


def _unpack_bf16_pairs(p):
    lo = lax.bitcast_convert_type(p << 16, F32)
    hi = lax.bitcast_convert_type(p & jnp.uint32(0xFFFF0000), F32)
    return jnp.concatenate([lo, hi], axis=1)


def _split_lanes(ref, x):
    width = ref.shape[2]
    for k in range(ref.shape[0]):
        ref[k] = x[:, k * width:(k + 1) * width].astype(ref.dtype)


def _join_lanes(ref):
    return jnp.concatenate([ref[k] for k in range(ref.shape[0])], axis=1)


def _sc_gather_pieces(x, idx):
    n = idx.shape[0]
    d = x.shape[1]
    win = SC_GATHER_WINDOW
    mesh = plsc.VectorSubcoreMesh(core_axis_name="c", subcore_axis_name="s")

    @functools.partial(pl.kernel, out_type=jax.ShapeDtypeStruct((n, d), x.dtype), mesh=mesh)
    def gather_kernel(x_hbm, i_hbm, o_hbm):
        def body(i_vmem, o_vmem):
            pltpu.sync_copy(x_hbm.at[i_vmem.at[0]], o_vmem)

        pltpu.emit_pipeline(
            body,
            grid=(n // win,),
            in_specs=[pl.BlockSpec((1, win), lambda i: (0, i))],
            out_specs=[pl.BlockSpec((win, d), lambda i: (i, 0))],
            core_axis_name=("c", "s"),
            dimension_semantics=(pltpu.PARALLEL,),
        )(i_hbm, o_hbm)

    return gather_kernel(x, idx.reshape(1, n))


def _routing_plan(info, t):
    tm = TM_MOE
    n_tiles = (2 * t) // tm + N_EXPERTS
    top = jnp.concatenate([info[:, 0], info[:, 1]]).astype(jnp.int32)
    onehot = (top[:, None] == jnp.arange(N_EXPERTS, dtype=jnp.int32)[None, :]).astype(jnp.int32)
    csum = jnp.cumsum(onehot, axis=0)
    rank = jnp.sum((csum - onehot) * onehot, axis=1)
    counts = csum[-1]
    ntile = (counts + tm - 1) // tm
    tile_end = jnp.cumsum(ntile)
    tile_start = tile_end - ntile
    pos = tile_start[top] * tm + rank
    n_used = tile_end[-1]
    tile_ids = jnp.arange(n_tiles, dtype=jnp.int32)
    te = jnp.sum((tile_ids[:, None] >= tile_end[None, :]).astype(jnp.int32), axis=1)
    last = jnp.sum((n_used - 1 >= tile_end).astype(jnp.int32))
    te = jnp.where(tile_ids < n_used, te, last).astype(jnp.int32)
    valid = jnp.clip(counts[te] - (tile_ids - tile_start[te]) * tm, 0, tm)
    valid = jnp.where(tile_ids < n_used, valid, 0).astype(jnp.int32)
    return te, n_used.reshape(1).astype(jnp.int32), valid, pos, n_tiles * tm


def _pad_cols(w, n):
    return jnp.pad(w, ((0, 0), (0, n - w.shape[1])))


def _block_diag(w):
    nb, bs, _ = w.shape
    eye = jnp.eye(nb, dtype=w.dtype)
    return jnp.einsum('nij,nm->nimj', w, eye).reshape(nb * bs, nb * bs)


def kernel(x, norm_mix, norm_ffn, norm_final, ev_w_in, ev_conv_w, ev_conv_b, ev_ga_w, ev_ga_b, ev_gx_w, ev_gx_b, ev_lambda, ev_f_b, ev_w_out, ev_ffn_w1, ev_ffn_w3, ev_ffn_w2, od_w_in, od_gate_w2, od_gate_b, od_head_norm, od_w_out, od_router, od_exp_w1, od_exp_w3, od_exp_w2):
    bsz, seq, d = x.shape
    t = bsz * seq
    depth = norm_mix.shape[0]
    h = x.reshape(t, d)
    n_even_main = 2 * LRU_WIDTH + 3 * FOX_HEADS * FOX_HEAD_DIM
    n_odd_main = 2 * GLA_HEADS * GLA_DK + 2 * GLA_HEADS * GLA_DV
    row = lambda v: v.reshape(1, -1)
    expert_w = None

    for layer in range(depth):
        j = layer // 2
        g_mix = row(norm_mix[layer])
        g_ffn = row(norm_ffn[layer])
        if layer % 2 == 0:
            w_in = ev_w_in[j]
            w = jnp.concatenate([w_in[:, :n_even_main], _pad_cols(w_in[:, n_even_main:], LANES)], axis=1).astype(BF16)
            xy, qkv, f = _proj(h, g_mix, w, [(0, 2 * LRU_WIDTH, F32), (2 * LRU_WIDTH, 1536, BF16), (n_even_main, LANES, F32)])
            fb = _pad_cols(row(ev_f_b[j]), LANES)
            ya, c = _lru(xy, f, ev_conv_w[j], row(ev_conv_b[j]),
                         _block_diag(ev_ga_w[j]).astype(BF16), _block_diag(ev_gx_w[j]).astype(BF16),
                         row(ev_ga_b[j]), row(ev_gx_b[j]), row(ev_lambda[j]), fb, bsz, seq)
            nxt = min(j, od_exp_w1.shape[0] - 1)
            yb, expert_w = _attn(*_attn_prep(qkv, c, bsz, seq), (od_exp_w1, od_exp_w3, od_exp_w2), nxt, bsz, seq)
            h = _ffn(h, ya, yb, ev_w_out[j].astype(BF16), g_ffn,
                     ev_ffn_w1[j].astype(BF16), ev_ffn_w3[j].astype(BF16), ev_ffn_w2[j].astype(BF16))
        else:
            w_in = od_w_in[j]
            w = jnp.concatenate([w_in[:, :n_odd_main], _pad_cols(w_in[:, n_odd_main:], LANES)], axis=1).astype(BF16)
            qk, v, g, lr = _proj(h, g_mix, w, [(0, 1024, F32), (1024, 1024, BF16), (2048, 1024, F32), (n_odd_main, LANES, F32)])
            gw = jnp.pad(od_gate_w2[j], ((0, LANES - GLA_RANK), (0, 0))).astype(BF16)
            o = _gla(qk, v, g, lr, gw, row(od_gate_b[j]), row(od_head_norm[j]), bsz, seq)
            rw = _pad_cols(od_router[j], LANES)
            h1, xn, info = _router(h, o, od_w_out[j].astype(BF16), g_ffn, rw)
            te, n_used, valid, pos, p_rows = _routing_plan(info, t)
            xs = _sc_scatter(xn, pos, p_rows)
            y = _moe(te, n_used, valid, xs, *expert_w)
            yg = _sc_gather(y, pos)
            final = layer == depth - 1
            h = _combine(h1, yg, info, row(norm_final), final)
    if depth % 2 == 1:
        raise NotImplementedError("final norm is fused into the last odd layer")
    return h.reshape(bsz, seq, d)
```

```python
import functools

import jax
import jax.numpy as jnp
import numpy as np
from jax import lax
from jax.experimental import pallas as pl
from jax.experimental.pallas import tpu as pltpu
from jax.experimental.pallas import tpu_sc as plsc

F32 = jnp.float32
BF16 = jnp.bfloat16

EPS = 1e-6
LRU_WIDTH = 512
LRU_BLOCKS = 8
CONV_W = 4
LRU_C = 8.0
FOX_HEADS = 8
FOX_HEAD_DIM = 64
GLA_HEADS = 4
GLA_DK = 128
GLA_DV = 256
GLA_RANK = 16
GLA_TAU = 16.0
GLA_CHUNK = 64
N_EXPERTS = 8
LANES = 128
SUBLANES = 8
LOG2E = 1.4426950408889634
NEG_BIG = -1e30
VMEM_LIMIT = 56 * 1024 * 1024
VMEM_LIMIT_MOE = 60 * 1024 * 1024

TM_PROJ = 1024
TC_LRU = 512
TQ_ATTN = 512
TM_FFN = 512
FC_FFN = 256
R_GLA = 2048
TM_ROUTER = 1024
TM_MOE = 256
FC_MOE = 512
TM_COMBINE = 1024
SC_GATHER_WINDOW = 128
PIECE = 256


def _cparams(*sem):
    return pltpu.CompilerParams(dimension_semantics=sem, vmem_limit_bytes=VMEM_LIMIT)


def _rmsnorm(x, g):
    return x * lax.rsqrt(jnp.mean(x * x, axis=-1, keepdims=True) + EPS) * g


def _sigmoid(x):
    return 1.0 / (1.0 + jnp.exp(-x))


def _log_sigmoid(z):
    return jnp.minimum(z, 0.0) - jnp.log1p(jnp.exp(-jnp.abs(z)))


def _dot(a, b):
    return jnp.dot(a, b, preferred_element_type=F32)


def _dot_nt(a, b):
    return lax.dot_general(a, b, (((1,), (1,)), ((), ())), preferred_element_type=F32)


def _dot_tn(a, b):
    return lax.dot_general(a, b, (((0,), (0,)), ((), ())), preferred_element_type=F32)


def _proj_kernel(h_ref, g_ref, w_ref, *out_refs, splits):
    xn = _rmsnorm(h_ref[...], g_ref[...]).astype(BF16)
    for o_ref, (start, width) in zip(out_refs, splits):
        o_ref[...] = _dot(xn, w_ref[:, start:start + width]).astype(o_ref.dtype)


def _proj(h, g, w, outs):
    t, d = h.shape
    n = w.shape[1]
    tm = TM_PROJ
    splits = tuple((s, wd) for s, wd, _ in outs)
    return pl.pallas_call(
        functools.partial(_proj_kernel, splits=splits),
        grid=(t // tm,),
        in_specs=[pl.BlockSpec((tm, d), lambda i: (i, 0)),
                  pl.BlockSpec((1, d), lambda i: (0, 0)),
                  pl.BlockSpec((d, n), lambda i: (0, 0))],
        out_specs=[pl.BlockSpec((tm, wd), lambda i: (i, 0)) for _, wd, _ in outs],
        out_shape=[jax.ShapeDtypeStruct((t, wd), dt) for _, wd, dt in outs],
        compiler_params=_cparams("parallel"),
        name="proj",
    )(h, g, w)


def _scan8(a, u):
    row = lax.broadcasted_iota(jnp.int32, u.shape, 0) & (SUBLANES - 1)
    for d in (1, 2, 4):
        m = row >= d
        u_sh = pltpu.roll(u, d, 0)
        if a is None:
            u = jnp.where(m, u + u_sh, u)
        else:
            a_sh = pltpu.roll(a, d, 0)
            u = jnp.where(m, a * u_sh + u, u)
            a = jnp.where(m, a * a_sh, a)
    return a, u


def _lru_kernel(xr_ref, yr_ref, f_ref, cw_ref, cb_ref, wga_ref, wgx_ref, gab_ref, gxb_ref,
                lam_ref, fb_ref, ya_ref, c_ref, xprev, hprev, cprev, hs, *, tc):
    @pl.when(pl.program_id(1) == 0)
    def _():
        xprev[...] = jnp.zeros_like(xprev)
        hprev[...] = jnp.zeros_like(hprev)
        cprev[...] = jnp.zeros_like(cprev)

    x = xr_ref[...]
    p8 = xprev[...]
    row8 = lax.broadcasted_iota(jnp.int32, p8.shape, 0)
    xc = cb_ref[...] + cw_ref[CONV_W - 1:CONV_W, :] * x
    for j in range(CONV_W - 1):
        sh = CONV_W - 1 - j
        xs = pltpu.roll(x, sh, 0)
        head = jnp.where(row8 < sh, pltpu.roll(p8, sh, 0), xs[:SUBLANES])
        xs = jnp.concatenate([head, xs[SUBLANES:]], axis=0)
        xc = xc + cw_ref[j:j + 1, :] * xs
    xprev[...] = x[tc - SUBLANES:, :]

    xb = xc.astype(BF16)
    r = _sigmoid(_dot(xb, wga_ref[...]) + gab_ref[...])
    i = _sigmoid(_dot(xb, wgx_ref[...]) + gxb_ref[...])
    nl = -lam_ref[...]
    softplus_nl = jnp.maximum(nl, 0.0) + jnp.log1p(jnp.exp(-jnp.abs(nl)))
    log_a = (-LRU_C) * r * softplus_nl
    a = jnp.exp(log_a)
    u = jnp.sqrt(-jnp.tanh(log_a) * (1.0 + a * a)) * (i * xc)

    a8, u8 = _scan8(a, u)
    _, lf8 = _scan8(None, _log_sigmoid(f_ref[...] + fb_ref[...]))

    h = hprev[...]
    c = cprev[...]
    for gi in range(tc // SUBLANES):
        rows = slice(gi * SUBLANES, (gi + 1) * SUBLANES)
        blk = a8[rows] * h + u8[rows]
        hs[rows, :] = blk
        h = blk[SUBLANES - 1:SUBLANES, :]
        cblk = lf8[rows] + c
        c_ref[rows, :] = cblk
        c = cblk[SUBLANES - 1:SUBLANES, :]
    hprev[...] = h
    cprev[...] = c

    yr = yr_ref[...]
    gelu = 0.5 * yr * (1.0 + jnp.tanh(0.7978845608028654 * (yr + 0.044715 * (yr * yr * yr))))
    ya_ref[...] = (hs[...] * gelu).astype(BF16)


def _lru(xy, f, cw, cb, wga, wgx, gab, gxb, lam, fb, bsz, seq):
    t = xy.shape[0]
    w = LRU_WIDTH
    tc = TC_LRU
    ns = seq // tc
    row = lambda b, s: (b * ns + s, 0)
    const = lambda b, s: (0, 0)
    return pl.pallas_call(
        functools.partial(_lru_kernel, tc=tc),
        grid=(bsz, ns),
        in_specs=[pl.BlockSpec((tc, w), row),
                  pl.BlockSpec((tc, w), lambda b, s: (b * ns + s, 1)),
                  pl.BlockSpec((tc, LANES), row),
                  pl.BlockSpec((CONV_W, w), const),
                  pl.BlockSpec((1, w), const),
                  pl.BlockSpec((w, w), const),
                  pl.BlockSpec((w, w), const),
                  pl.BlockSpec((1, w), const),
                  pl.BlockSpec((1, w), const),
                  pl.BlockSpec((1, w), const),
                  pl.BlockSpec((1, LANES), const)],
        out_specs=[pl.BlockSpec((tc, w), row), pl.BlockSpec((tc, LANES), row)],
        out_shape=[jax.ShapeDtypeStruct((t, w), BF16), jax.ShapeDtypeStruct((t, LANES), F32)],
        scratch_shapes=[pltpu.VMEM((SUBLANES, w), F32), pltpu.VMEM((1, w), F32),
                        pltpu.VMEM((1, LANES), F32), pltpu.VMEM((tc, w), F32)],
        compiler_params=_cparams("parallel", "arbitrary"),
        name="lru",
    )(xy, xy, f, cw, cb, wga, wgx, gab, gxb, lam, fb)


def _placement_matrices():
    nh, hd = FOX_HEADS, FOX_HEAD_DIM
    w = nh * hd
    pq = np.zeros((nh * LANES, w + 3 * LANES), np.float32)
    pk = np.zeros((w + 3 * LANES, nh * LANES), np.float32)
    for h in range(nh):
        for j in range(hd):
            pq[h * LANES + j, h * hd + j] = 1.0
            pk[h * hd + j, h * LANES + j] = 1.0
        for part in range(3):
            pq[h * LANES + hd + part, w + part * LANES + h] = 1.0
            pk[w + part * LANES + h, h * LANES + hd + 3 + part] = -1.0
    return jnp.asarray(pq, BF16), jnp.asarray(pk, BF16), jnp.eye(w, dtype=BF16)


def _attn_prep_kernel(qkv_ref, c_ref, pq_ref, pk_ref, eye_ref, qt_ref, k_ref, vt_ref):
    hd = FOX_HEAD_DIM
    w = FOX_HEADS * hd
    c = c_ref[...] * LOG2E
    hi = c.astype(BF16)
    r1 = c - hi.astype(F32)
    mid = r1.astype(BF16)
    lo = (r1 - mid.astype(F32)).astype(BF16)
    cs = jnp.concatenate([hi, mid, lo], axis=1)
    qkv = qkv_ref[...]
    q = (qkv[:, :w].astype(F32) * (hd ** -0.5 * LOG2E)).astype(BF16)
    xq = jnp.concatenate([q, cs], axis=1)
    xk = jnp.concatenate([qkv[:, w:2 * w], cs], axis=1)
    qt = _dot_nt(pq_ref[...], xq)
    feat = lax.broadcasted_iota(jnp.int32, (qt.shape[0], 1), 0) & (LANES - 1)
    qt = qt + jnp.where((feat >= hd + 3) & (feat < hd + 6), 1.0, 0.0)
    for h in range(FOX_HEADS):
        qt_ref[h] = qt[h * LANES:(h + 1) * LANES, :].astype(BF16)
    ka = _dot(xk, pk_ref[...])
    col = lax.broadcasted_iota(jnp.int32, (1, ka.shape[1]), 1) & (LANES - 1)
    k_ref[...] = (ka + jnp.where((col >= hd) & (col < hd + 3), 1.0, 0.0)).astype(BF16)
    vt = _dot_nt(eye_ref[...], qkv[:, 2 * w:])
    for hp in range(FOX_HEADS // 2):
        vt_ref[hp, 0] = vt[hp * LANES:(hp + 1) * LANES, :].astype(BF16)


def _attn_prep(qkv, c, bsz, seq):
    nh = FOX_HEADS
    tb = min(TQ_ATTN, seq)
    ns = seq // tb
    pq, pk, eye = _placement_matrices()
    const = lambda b, s: (0, 0)
    return pl.pallas_call(
        _attn_prep_kernel,
        grid=(bsz, ns),
        in_specs=[pl.BlockSpec((tb, qkv.shape[1]), lambda b, s: (b * ns + s, 0)),
                  pl.BlockSpec((tb, LANES), lambda b, s: (b * ns + s, 0)),
                  pl.BlockSpec(pq.shape, const), pl.BlockSpec(pk.shape, const), pl.BlockSpec(eye.shape, const)],
        out_specs=[pl.BlockSpec((None, nh, LANES, tb), lambda b, s: (b, 0, 0, s)),
                   pl.BlockSpec((tb, nh * LANES), lambda b, s: (b * ns + s, 0)),
                   pl.BlockSpec((None, nh // 2, 1, LANES, tb), lambda b, s: (b, 0, s, 0, 0))],
        out_shape=[jax.ShapeDtypeStruct((bsz, nh, LANES, seq), BF16),
                   jax.ShapeDtypeStruct((bsz * seq, nh * LANES), BF16),
                   jax.ShapeDtypeStruct((bsz, nh // 2, ns, LANES, tb), BF16)],
        compiler_params=_cparams("parallel", "parallel"),
        name="attn_prep",
    )(qkv, c, pq, pk, eye)


def _attn_kernel(qt_ref, k_ref, vt_ref, wa_ref, wb_ref, wc_ref, o_ref, wa_out, wb_out, wc_out,
                 m_sc, l_sc, acc_sc, s_sc, *, tk):
    wa_out[...] = wa_ref[...].astype(BF16)
    wb_out[...] = wb_ref[...].astype(BF16)
    wc_out[...] = wc_ref[...].astype(BF16)
    qi = pl.program_id(2)
    m_sc[...] = jnp.full_like(m_sc, NEG_BIG)
    l_sc[...] = jnp.zeros_like(l_sc)
    acc_sc[...] = jnp.zeros_like(acc_sc)

    def logits(kb, hh):
        k2 = k_ref[pl.ds(pl.multiple_of(kb * tk, tk), tk), hh * LANES:(hh + 1) * LANES]
        return _dot(k2, qt_ref[hh])

    def update(kb, hh, s, masked):
        if masked:
            key = lax.broadcasted_iota(jnp.int32, s.shape, 0)
            qry = lax.broadcasted_iota(jnp.int32, s.shape, 1)
            s = jnp.where(key <= qry, s, NEG_BIG)
        m_prev = m_sc[hh]
        m_new = jnp.maximum(m_prev, jnp.max(s, axis=0, keepdims=True))
        alpha = jnp.exp2(m_prev - m_new)
        p = jnp.exp2(s - m_new)
        l_sc[hh] = alpha * l_sc[hh] + jnp.sum(p, axis=0, keepdims=True)
        acc_sc[hh] = alpha * acc_sc[hh] + _dot(vt_ref[kb], p.astype(BF16))
        m_sc[hh] = m_new

    for hh in range(2):
        s_sc[hh] = logits(0, hh)

    def body(kb, carry):
        for hh in range(2):
            s = s_sc[hh]
            s_next = logits(kb + 1, hh)
            update(kb, hh, s, False)
            s_sc[hh] = s_next
        return carry

    lax.fori_loop(0, qi, body, 0)
    for hh in range(2):
        update(qi, hh, s_sc[hh], True)
    feat = lax.broadcasted_iota(jnp.int32, (LANES, 1), 0)
    ot = jnp.where(feat < FOX_HEAD_DIM, acc_sc[0] / l_sc[0], acc_sc[1] / l_sc[1])
    o_ref[...] = ot.T.astype(o_ref.dtype)


def _attn(qt, k_aug, vt, weights, layer, bsz, seq):
    nk, tk = vt.shape[2], vt.shape[4]
    tq = tk
    nq = seq // tq
    npair = FOX_HEADS // 2
    steps = bsz * npair * nq
    flat = [w.reshape(-1, w.shape[-1]) for w in weights]
    rows = [w.shape[0] // (weights[0].shape[0] * steps) for w in flat]
    step = lambda b, h, qi: (b * npair + h) * nq + qi
    in_w = [pl.BlockSpec((r, w.shape[1]), lambda b, h, qi: (layer * steps + step(b, h, qi), 0))
            for r, w in zip(rows, flat)]
    out_w = [pl.BlockSpec((r, w.shape[1]), lambda b, h, qi: (step(b, h, qi), 0)) for r, w in zip(rows, flat)]
    outs = pl.pallas_call(
        functools.partial(_attn_kernel, tk=tk),
        grid=(bsz, npair, nq),
        in_specs=[pl.BlockSpec((None, 2, LANES, tq), lambda b, h, qi: (b, h, 0, qi)),
                  pl.BlockSpec((seq, 2 * LANES), lambda b, h, qi: (b, h)),
                  pl.BlockSpec((None, None, nk, LANES, tk), lambda b, h, qi: (b, h, 0, 0, 0))] + in_w,
        out_specs=[pl.BlockSpec((tq, LANES), lambda b, h, qi: (b * nq + qi, h))] + out_w,
        out_shape=[jax.ShapeDtypeStruct((bsz * seq, FOX_HEADS * FOX_HEAD_DIM), BF16)]
        + [jax.ShapeDtypeStruct((r * steps, w.shape[1]), BF16) for r, w in zip(rows, flat)],
        scratch_shapes=[pltpu.VMEM((2, 1, tq), F32), pltpu.VMEM((2, 1, tq), F32),
                        pltpu.VMEM((2, LANES, tq), F32), pltpu.VMEM((2, tk, tq), F32)],
        compiler_params=_cparams("parallel", "parallel", "arbitrary"),
        name="attn",
    )(qt, k_aug, vt, *flat)
    return outs[0], [o.reshape(w.shape[1:]) for o, w in zip(outs[1:], weights)]


def _swiglu(xn, w1_ref, w3_ref, w2_ref, h_sc, fc):
    f = w1_ref.shape[1]
    for c0 in range(0, f, fc):
        a = _dot(xn, w1_ref[:, c0:c0 + fc])
        b = _dot(xn, w3_ref[:, c0:c0 + fc])
        h_sc[:, c0:c0 + fc] = (a * _sigmoid(a) * b).astype(BF16)
    return _dot(h_sc[...], w2_ref[...])


def _ffn_kernel(h_ref, ya_ref, yb_ref, wo_ref, g_ref, w1_ref, w3_ref, w2_ref, o_ref, h_sc):
    half = ya_ref.shape[1]
    h1 = h_ref[...] + _dot(ya_ref[...], wo_ref[:half, :]) + _dot(yb_ref[...], wo_ref[half:, :])
    xn = _rmsnorm(h1, g_ref[...]).astype(BF16)
    o_ref[...] = h1 + _swiglu(xn, w1_ref, w3_ref, w2_ref, h_sc, FC_FFN)


def _ffn(h, ya, yb, wo, g, w1, w3, w2):
    t, d = h.shape
    tm = TM_FFN
    f = w1.shape[1]
    half = ya.shape[1]
    row = lambda i: (i, 0)
    c2 = lambda i: (0, 0)
    return pl.pallas_call(
        _ffn_kernel,
        grid=(t // tm,),
        in_specs=[pl.BlockSpec((tm, d), row), pl.BlockSpec((tm, half), row), pl.BlockSpec((tm, half), row),
                  pl.BlockSpec((d, d), c2), pl.BlockSpec((1, d), c2),
                  pl.BlockSpec((d, f), c2), pl.BlockSpec((d, f), c2), pl.BlockSpec((f, d), c2)],
        out_specs=pl.BlockSpec((tm, d), row),
        out_shape=jax.ShapeDtypeStruct((t, d), F32),
        scratch_shapes=[pltpu.VMEM((tm, f), BF16)],
        compiler_params=_cparams("parallel"),
        name="ffn",
    )(h, ya, yb, wo, g, w1, w3, w2)


def _gla_kernel(q_ref, k_ref, v_ref, g_ref, lr_ref, gw_ref, gb_ref, hn_ref, o_ref, state_t, *, rows):
    @pl.when(pl.program_id(2) == 0)
    def _():
        state_t[...] = jnp.zeros_like(state_t)

    cs = GLA_CHUNK
    log_a = _log_sigmoid(_dot(lr_ref[...].astype(BF16), gw_ref[...]) + gb_ref[...]) * (1.0 / GLA_TAU)
    rowi = lax.broadcasted_iota(jnp.int32, (cs, GLA_DK), 0)
    tri = (lax.broadcasted_iota(jnp.int32, (cs, cs), 0) >= lax.broadcasted_iota(jnp.int32, (cs, cs), 1))
    for ci in range(rows // cs):
        sl = slice(ci * cs, (ci + 1) * cs)
        b = log_a[sl]
        d = 1
        while d < cs:
            b = b + jnp.where(rowi >= d, pltpu.roll(b, d, 0), 0.0)
            d *= 2
        b_last = b[cs - 1:cs, :]
        q = q_ref[sl, :] * (GLA_DK ** -0.5)
        k = k_ref[sl, :]
        v = v_ref[sl, :]
        q_dec = (q * jnp.exp(b)).astype(BF16)
        k_intra = (k * jnp.exp(-b)).astype(BF16)
        k_state = (k * jnp.exp(b_last - b)).astype(BF16)
        scores = jnp.where(tri, _dot_nt(q_dec, k_intra), 0.0)
        st = state_t[...]
        o = _dot(scores.astype(BF16), v) + _dot_nt(q_dec, st.astype(BF16))
        state_t[...] = st * jnp.exp(b_last) + _dot_tn(v, k_state)
        o = o * lax.rsqrt(jnp.mean(o * o, axis=-1, keepdims=True) + EPS) * hn_ref[...]
        g = g_ref[sl, :]
        o_ref[sl, :] = (o * (g * _sigmoid(g))).astype(o_ref.dtype)


def _gla(qk, v, g, lr, gw, gb, hn, bsz, seq):
    t = qk.shape[0]
    rows = min(R_GLA, seq)
    nr = seq // rows
    nh = GLA_HEADS
    return pl.pallas_call(
        functools.partial(_gla_kernel, rows=rows),
        grid=(bsz, nh, nr),
        in_specs=[pl.BlockSpec((rows, GLA_DK), lambda b, h, r: (b * nr + r, h)),
                  pl.BlockSpec((rows, GLA_DK), lambda b, h, r: (b * nr + r, nh + h)),
                  pl.BlockSpec((rows, GLA_DV), lambda b, h, r: (b * nr + r, h)),
                  pl.BlockSpec((rows, GLA_DV), lambda b, h, r: (b * nr + r, h)),
                  pl.BlockSpec((rows, LANES), lambda b, h, r: (b * nr + r, 0)),
                  pl.BlockSpec((LANES, GLA_DK), lambda b, h, r: (0, h)),
                  pl.BlockSpec((1, GLA_DK), lambda b, h, r: (0, h)),
                  pl.BlockSpec((1, GLA_DV), lambda b, h, r: (0, h))],
        out_specs=pl.BlockSpec((rows, GLA_DV), lambda b, h, r: (b * nr + r, h)),
        out_shape=jax.ShapeDtypeStruct((t, nh * GLA_DV), BF16),
        scratch_shapes=[pltpu.VMEM((GLA_DV, GLA_DK), F32)],
        compiler_params=_cparams("parallel", "parallel", "arbitrary"),
        name="gla",
    )(qk, qk, v, g, lr, gw, gb, hn)


def _router_kernel(h_ref, o_ref, wo_ref, g_ref, rw_ref, h1_ref, xn_ref, info_ref):
    h1 = h_ref[...] + _dot(o_ref[...], wo_ref[...])
    h1_ref[...] = h1
    xn = _rmsnorm(h1, g_ref[...])
    _split_lanes(xn_ref, _pack_bf16_pairs(xn))
    rw = rw_ref[...]
    xh = xn.astype(BF16)
    xl = (xn - xh.astype(F32)).astype(BF16)
    rh = rw.astype(BF16)
    rl = (rw - rh.astype(F32)).astype(BF16)
    logits = _dot(xh, rh) + (_dot(xh, rl) + _dot(xl, rh))
    lane = lax.broadcasted_iota(jnp.int32, logits.shape, 1)
    lg = jnp.where(lane < N_EXPERTS, logits, -jnp.inf)
    m1 = jnp.max(lg, axis=-1, keepdims=True)
    i1 = jnp.min(jnp.where(lg == m1, lane, LANES), axis=-1, keepdims=True)
    lg2 = jnp.where(lane == i1, -jnp.inf, lg)
    m2 = jnp.max(lg2, axis=-1, keepdims=True)
    i2 = jnp.min(jnp.where(lg2 == m2, lane, LANES), axis=-1, keepdims=True)
    e = jnp.exp(m2 - m1)
    w1 = 1.0 / (1.0 + e)
    w2 = e * w1
    info = jnp.where(lane == 0, i1.astype(F32),
                     jnp.where(lane == 1, i2.astype(F32),
                               jnp.where(lane == 2, w1, jnp.where(lane == 3, w2, 0.0))))
    info_ref[...] = info


def _router(h, o, wo, g, rw):
    t, d = h.shape
    tm = TM_ROUTER
    row = lambda i: (i, 0)
    c2 = lambda i: (0, 0)
    return pl.pallas_call(
        _router_kernel,
        grid=(t // tm,),
        in_specs=[pl.BlockSpec((tm, d), row), pl.BlockSpec((tm, d), row), pl.BlockSpec((d, d), c2),
                  pl.BlockSpec((1, d), c2), pl.BlockSpec((d, LANES), c2)],
        out_specs=[pl.BlockSpec((tm, d), row), pl.BlockSpec((d // (2 * PIECE), tm, PIECE), lambda i: (0, i, 0)),
                   pl.BlockSpec((tm, LANES), row)],
        out_shape=[jax.ShapeDtypeStruct((t, d), F32), jax.ShapeDtypeStruct((d // (2 * PIECE), t, PIECE), jnp.uint32),
                   jax.ShapeDtypeStruct((t, LANES), F32)],
        compiler_params=_cparams("parallel"),
        name="router",
    )(h, o, wo, g, rw)


def _moe_kernel(te_ref, nu_ref, valid_ref, xs_ref, w1_ref, w3_ref, w2_ref, y_ref, h_sc):
    i = pl.program_id(0)

    @pl.when(i < nu_ref[0])
    def _():
        packed = _join_lanes(xs_ref)
        row = lax.broadcasted_iota(jnp.int32, packed.shape, 0)
        packed = jnp.where(row < valid_ref[i], packed, jnp.uint32(0))
        x = _unpack_bf16_pairs(packed).astype(BF16)
        _split_lanes(y_ref, _pack_bf16_pairs(_swiglu(x, w1_ref, w3_ref, w2_ref, h_sc, FC_MOE)))

    @pl.when(i >= nu_ref[0])
    def _():
        y_ref[...] = jnp.zeros_like(y_ref)


def _moe(tile_expert, n_used, valid, xs, w1, w3, w2):
    pieces, p, width = xs.shape
    tm = TM_MOE
    d, f = w1.shape[1:]
    grid_spec = pltpu.PrefetchScalarGridSpec(
        num_scalar_prefetch=3,
        grid=(p // tm,),
        in_specs=[pl.BlockSpec((pieces, tm, width), lambda i, te, nu, va: (0, i, 0)),
                  pl.BlockSpec((None, d, f), lambda i, te, nu, va: (te[i], 0, 0)),
                  pl.BlockSpec((None, d, f), lambda i, te, nu, va: (te[i], 0, 0)),
                  pl.BlockSpec((None, f, d), lambda i, te, nu, va: (te[i], 0, 0))],
        out_specs=pl.BlockSpec((pieces, tm, width), lambda i, te, nu, va: (0, i, 0)),
        scratch_shapes=[pltpu.VMEM((tm, f), BF16)],
    )
    return pl.pallas_call(
        _moe_kernel,
        grid_spec=grid_spec,
        out_shape=jax.ShapeDtypeStruct((pieces, p, width), jnp.uint32),
        compiler_params=pltpu.CompilerParams(dimension_semantics=("arbitrary",), vmem_limit_bytes=VMEM_LIMIT_MOE),
        name="moe",
    )(tile_expert, n_used, valid, xs, w1, w3, w2)


def _combine_kernel(h1_ref, y1_ref, y2_ref, info_ref, gf_ref, o_ref, *, final):
    info = info_ref[...]
    y1 = _unpack_bf16_pairs(_join_lanes(y1_ref))
    y2 = _unpack_bf16_pairs(_join_lanes(y2_ref))
    out = h1_ref[...] + info[:, 2:3] * y1 + info[:, 3:4] * y2
    if final:
        out = _rmsnorm(out, gf_ref[...])
    o_ref[...] = out


def _combine(h1, yg, info, gf, final):
    t, d = h1.shape
    tm = TM_COMBINE
    nt = t // tm
    return pl.pallas_call(
        functools.partial(_combine_kernel, final=final),
        grid=(nt,),
        in_specs=[pl.BlockSpec((tm, d), lambda i: (i, 0)),
                  pl.BlockSpec((d // (2 * PIECE), tm, PIECE), lambda i: (0, i, 0)),
                  pl.BlockSpec((d // (2 * PIECE), tm, PIECE), lambda i: (0, nt + i, 0)),
                  pl.BlockSpec((tm, LANES), lambda i: (i, 0)),
                  pl.BlockSpec((1, d), lambda i: (0, 0))],
        out_specs=pl.BlockSpec((tm, d), lambda i: (i, 0)),
        out_shape=jax.ShapeDtypeStruct((t, d), F32),
        compiler_params=_cparams("parallel"),
        name="combine",
    )(h1, yg, yg, info, gf)


def _sc_gather(x, idx):
    pieces, rows, lanes = x.shape
    idx = (jnp.arange(pieces, dtype=jnp.int32)[:, None] * rows + idx[None, :]).reshape(-1)
    out = _sc_gather_pieces(x.reshape(pieces * rows, lanes), idx)
    return out.reshape(pieces, -1, lanes)


def _sc_scatter(x, pos, out_rows):
    pieces, t, width = x.shape
    idx = (jnp.arange(pieces, dtype=jnp.int32)[:, None] * out_rows + pos[None, :]).reshape(-1)
    out = _sc_scatter_pieces(x.reshape(pieces * t, width), idx, pos.shape[0], pieces * out_rows)
    return out.reshape(pieces, out_rows, width)


def _sc_scatter_pieces(x, idx, slots, out_rows):
    n = idx.shape[0]
    d = x.shape[1]
    win = SC_GATHER_WINDOW
    pieces = n // slots
    nb_slots = slots // win
    nb_rows = x.shape[0] // pieces // win
    mesh = plsc.VectorSubcoreMesh(core_axis_name="c", subcore_axis_name="s")

    @functools.partial(pl.kernel, out_type=jax.ShapeDtypeStruct((out_rows, d), x.dtype), mesh=mesh)
    def scatter_kernel(x_hbm, i_hbm, o_hbm):
        def body(x_vmem, i_vmem):
            pltpu.sync_copy(x_vmem, o_hbm.at[i_vmem.at[0]])

        pltpu.emit_pipeline(
            body,
            grid=(n // win,),
            in_specs=[pl.BlockSpec((win, d), lambda i: ((i // nb_slots) * nb_rows + (i % nb_slots) % nb_rows, 0)),
                      pl.BlockSpec((1, win), lambda i: (0, i))],
            out_specs=[],
            core_axis_name=("c", "s"),
            dimension_semantics=(pltpu.PARALLEL,),
        )(x_hbm, i_hbm)

    return scatter_kernel(x, idx.reshape(1, n))


def _bf16_bits(a):
    bits = lax.bitcast_convert_type(a, jnp.uint32)
    return (bits + (jnp.uint32(0x7FFF) + ((bits >> 16) & jnp.uint32(1)))) >> 16


def _pack_bf16_pairs(x):
    w = x.shape[1] // 2
    return _bf16_bits(x[:, :w]) | (_bf16_bits(x[:, w:]) << 16)


def _unpack_bf16_pairs(p):
    lo = lax.bitcast_convert_type(p << 16, F32)
    hi = lax.bitcast_convert_type(p & jnp.uint32(0xFFFF0000), F32)
    return jnp.concatenate([lo, hi], axis=1)


def _split_lanes(ref, x):
    width = ref.shape[2]
    for k in range(ref.shape[0]):
        ref[k] = x[:, k * width:(k + 1) * width].astype(ref.dtype)


def _join_lanes(ref):
    return jnp.concatenate([ref[k] for k in range(ref.shape[0])], axis=1)


def _sc_gather_pieces(x, idx):
    n = idx.shape[0]
    d = x.shape[1]
    win = SC_GATHER_WINDOW
    mesh = plsc.VectorSubcoreMesh(core_axis_name="c", subcore_axis_name="s")

    @functools.partial(pl.kernel, out_type=jax.ShapeDtypeStruct((n, d), x.dtype), mesh=mesh)
    def gather_kernel(x_hbm, i_hbm, o_hbm):
        def body(i_vmem, o_vmem):
            pltpu.sync_copy(x_hbm.at[i_vmem.at[0]], o_vmem)

        pltpu.emit_pipeline(
            body,
            grid=(n // win,),
            in_specs=[pl.BlockSpec((1, win), lambda i: (0, i))],
            out_specs=[pl.BlockSpec((win, d), lambda i: (i, 0))],
            core_axis_name=("c", "s"),
            dimension_semantics=(pltpu.PARALLEL,),
        )(i_hbm, o_hbm)

    return gather_kernel(x, idx.reshape(1, n))


def _routing_plan(info, t):
    tm = TM_MOE
    n_tiles = (2 * t) // tm + N_EXPERTS
    top = jnp.concatenate([info[:, 0], info[:, 1]]).astype(jnp.int32)
    onehot = (top[:, None] == jnp.arange(N_EXPERTS, dtype=jnp.int32)[None, :]).astype(jnp.int32)
    csum = jnp.cumsum(onehot, axis=0)
    rank = jnp.sum((csum - onehot) * onehot, axis=1)
    counts = csum[-1]
    ntile = (counts + tm - 1) // tm
    tile_end = jnp.cumsum(ntile)
    tile_start = tile_end - ntile
    pos = tile_start[top] * tm + rank
    n_used = tile_end[-1]
    tile_ids = jnp.arange(n_tiles, dtype=jnp.int32)
    te = jnp.sum((tile_ids[:, None] >= tile_end[None, :]).astype(jnp.int32), axis=1)
    last = jnp.sum((n_used - 1 >= tile_end).astype(jnp.int32))
    te = jnp.where(tile_ids < n_used, te, last).astype(jnp.int32)
    valid = jnp.clip(counts[te] - (tile_ids - tile_start[te]) * tm, 0, tm)
    valid = jnp.where(tile_ids < n_used, valid, 0).astype(jnp.int32)
    return te, n_used.reshape(1).astype(jnp.int32), valid, pos, n_tiles * tm


def _pad_cols(w, n):
    return jnp.pad(w, ((0, 0), (0, n - w.shape[1])))


def _block_diag(w):
    nb, bs, _ = w.shape
    eye = jnp.eye(nb, dtype=w.dtype)
    return jnp.einsum('nij,nm->nimj', w, eye).reshape(nb * bs, nb * bs)


def kernel(x, norm_mix, norm_ffn, norm_final, ev_w_in, ev_conv_w, ev_conv_b, ev_ga_w, ev_ga_b, ev_gx_w, ev_gx_b, ev_lambda, ev_f_b, ev_w_out, ev_ffn_w1, ev_ffn_w3, ev_ffn_w2, od_w_in, od_gate_w2, od_gate_b, od_head_norm, od_w_out, od_router, od_exp_w1, od_exp_w3, od_exp_w2):
    bsz, seq, d = x.shape
    t = bsz * seq
    depth = norm_mix.shape[0]
    h = x.reshape(t, d)
    n_qkv = 3 * FOX_HEADS * FOX_HEAD_DIM
    n_gla_k = 2 * GLA_HEADS * GLA_DK
    n_gla_v = GLA_HEADS * GLA_DV
    n_even_main = 2 * LRU_WIDTH + n_qkv
    n_odd_main = n_gla_k + 2 * n_gla_v
    row = lambda v: v.reshape(1, -1)
    expert_w = None

    for layer in range(depth):
        j = layer // 2
        g_mix = row(norm_mix[layer])
        g_ffn = row(norm_ffn[layer])
        if layer % 2 == 0:
            w_in = ev_w_in[j]
            w = jnp.concatenate([w_in[:, :n_even_main], _pad_cols(w_in[:, n_even_main:], LANES)], axis=1).astype(BF16)
            xy, qkv, f = _proj(h, g_mix, w, [(0, 2 * LRU_WIDTH, F32), (2 * LRU_WIDTH, n_qkv, BF16),
                                             (n_even_main, LANES, F32)])
            fb = _pad_cols(row(ev_f_b[j]), LANES)
            ya, c = _lru(xy, f, ev_conv_w[j], row(ev_conv_b[j]),
                         _block_diag(ev_ga_w[j]).astype(BF16), _block_diag(ev_gx_w[j]).astype(BF16),
                         row(ev_ga_b[j]), row(ev_gx_b[j]), row(ev_lambda[j]), fb, bsz, seq)
            nxt = min(j, od_exp_w1.shape[0] - 1)
            yb, expert_w = _attn(*_attn_prep(qkv, c, bsz, seq), (od_exp_w1, od_exp_w3, od_exp_w2), nxt, bsz, seq)
            h = _ffn(h, ya, yb, ev_w_out[j].astype(BF16), g_ffn,
                     ev_ffn_w1[j].astype(BF16), ev_ffn_w3[j].astype(BF16), ev_ffn_w2[j].astype(BF16))
        else:
            w_in = od_w_in[j]
            w = jnp.concatenate([w_in[:, :n_odd_main], _pad_cols(w_in[:, n_odd_main:], LANES)], axis=1).astype(BF16)
            qk, v, g, lr = _proj(h, g_mix, w, [(0, n_gla_k, F32), (n_gla_k, n_gla_v, BF16),
                                               (n_gla_k + n_gla_v, n_gla_v, F32), (n_odd_main, LANES, F32)])
            gw = jnp.pad(od_gate_w2[j], ((0, LANES - GLA_RANK), (0, 0))).astype(BF16)
            o = _gla(qk, v, g, lr, gw, row(od_gate_b[j]), row(od_head_norm[j]), bsz, seq)
            rw = _pad_cols(od_router[j], LANES)
            h1, xn, info = _router(h, o, od_w_out[j].astype(BF16), g_ffn, rw)
            te, n_used, valid, pos, p_rows = _routing_plan(info, t)
            xs = _sc_scatter(xn, pos, p_rows)
            y = _moe(te, n_used, valid, xs, *expert_w)
            yg = _sc_gather(y, pos)
            final = layer == depth - 1
            h = _combine(h1, yg, info, row(norm_final), final)
    if depth % 2 == 1:
        raise NotImplementedError("final norm is fused into the last odd layer")
    return h.reshape(bsz, seq, d)
```

```python
import functools

import jax
import jax.numpy as jnp
import numpy as np
from jax import lax
from jax.experimental import pallas as pl
from jax.experimental.pallas import tpu as pltpu
from jax.experimental.pallas import tpu_sc as plsc

F32 = jnp.float32
BF16 = jnp.bfloat16

EPS = 1e-6
LRU_WIDTH = 512
LRU_BLOCKS = 8
CONV_W = 4
LRU_C = 8.0
FOX_HEADS = 8
FOX_HEAD_DIM = 64
GLA_HEADS = 4
GLA_DK = 128
GLA_DV = 256
GLA_RANK = 16
GLA_TAU = 16.0
GLA_CHUNK = 64
N_EXPERTS = 8
LANES = 128
SUBLANES = 8
LOG2E = 1.4426950408889634
NEG_BIG = -1e30
VMEM_LIMIT = 56 * 1024 * 1024
VMEM_LIMIT_MOE = 60 * 1024 * 1024

TM_PROJ = 1024
TC_LRU = 512
TQ_ATTN = 512
TM_FFN = 512
FC_FFN = 256
R_GLA = 2048
TM_ROUTER = 1024
TM_MOE = 256
FC_MOE = 512
TM_COMBINE = 1024
SC_GATHER_WINDOW = 128
PIECE = 256


def _cparams(*sem):
    return pltpu.CompilerParams(dimension_semantics=sem, vmem_limit_bytes=VMEM_LIMIT)


def _rmsnorm(x, g):
    return x * lax.rsqrt(jnp.mean(x * x, axis=-1, keepdims=True) + EPS) * g


def _sigmoid(x):
    return 1.0 / (1.0 + jnp.exp(-x))


def _log_sigmoid(z):
    return jnp.minimum(z, 0.0) - jnp.log1p(jnp.exp(-jnp.abs(z)))


def _dot(a, b):
    return jnp.dot(a, b, preferred_element_type=F32)


def _dot_nt(a, b):
    return lax.dot_general(a, b, (((1,), (1,)), ((), ())), preferred_element_type=F32)


def _dot_tn(a, b):
    return lax.dot_general(a, b, (((0,), (0,)), ((), ())), preferred_element_type=F32)


def _proj_kernel(h_ref, g_ref, w_ref, *out_refs, splits):
    xn = _rmsnorm(h_ref[...], g_ref[...]).astype(BF16)
    for o_ref, (start, width) in zip(out_refs, splits):
        o_ref[...] = _dot(xn, w_ref[:, start:start + width]).astype(o_ref.dtype)


def _proj(h, g, w, outs):
    t, d = h.shape
    n = w.shape[1]
    tm = TM_PROJ
    splits = tuple((s, wd) for s, wd, _ in outs)
    return pl.pallas_call(
        functools.partial(_proj_kernel, splits=splits),
        grid=(t // tm,),
        in_specs=[pl.BlockSpec((tm, d), lambda i: (i, 0)),
                  pl.BlockSpec((1, d), lambda i: (0, 0)),
                  pl.BlockSpec((d, n), lambda i: (0, 0))],
        out_specs=[pl.BlockSpec((tm, wd), lambda i: (i, 0)) for _, wd, _ in outs],
        out_shape=[jax.ShapeDtypeStruct((t, wd), dt) for _, wd, dt in outs],
        compiler_params=_cparams("parallel"),
        name="proj",
    )(h, g, w)


def _scan8(a, u):
    row = lax.broadcasted_iota(jnp.int32, u.shape, 0) & (SUBLANES - 1)
    for d in (1, 2, 4):
        m = row >= d
        u_sh = pltpu.roll(u, d, 0)
        if a is None:
            u = jnp.where(m, u + u_sh, u)
        else:
            a_sh = pltpu.roll(a, d, 0)
            u = jnp.where(m, a * u_sh + u, u)
            a = jnp.where(m, a * a_sh, a)
    return a, u


def _lru_kernel(xr_ref, yr_ref, f_ref, cw_ref, cb_ref, wga_ref, wgx_ref, gab_ref, gxb_ref,
                lam_ref, fb_ref, ya_ref, c_ref, xprev, hprev, cprev, hs, *, tc):
    @pl.when(pl.program_id(1) == 0)
    def _():
        xprev[...] = jnp.zeros_like(xprev)
        hprev[...] = jnp.zeros_like(hprev)
        cprev[...] = jnp.zeros_like(cprev)

    x = xr_ref[...]
    p8 = xprev[...]
    row8 = lax.broadcasted_iota(jnp.int32, p8.shape, 0)
    xc = cb_ref[...] + cw_ref[CONV_W - 1:CONV_W, :] * x
    for j in range(CONV_W - 1):
        sh = CONV_W - 1 - j
        xs = pltpu.roll(x, sh, 0)
        head = jnp.where(row8 < sh, pltpu.roll(p8, sh, 0), xs[:SUBLANES])
        xs = jnp.concatenate([head, xs[SUBLANES:]], axis=0)
        xc = xc + cw_ref[j:j + 1, :] * xs
    xprev[...] = x[tc - SUBLANES:, :]

    xb = xc.astype(BF16)
    r = _sigmoid(_dot(xb, wga_ref[...]) + gab_ref[...])
    i = _sigmoid(_dot(xb, wgx_ref[...]) + gxb_ref[...])
    nl = -lam_ref[...]
    softplus_nl = jnp.maximum(nl, 0.0) + jnp.log1p(jnp.exp(-jnp.abs(nl)))
    log_a = (-LRU_C) * r * softplus_nl
    a = jnp.exp(log_a)
    u = jnp.sqrt(-jnp.tanh(log_a) * (1.0 + a * a)) * (i * xc)

    a8, u8 = _scan8(a, u)
    _, lf8 = _scan8(None, _log_sigmoid(f_ref[...] + fb_ref[...]))

    h = hprev[...]
    c = cprev[...]
    for gi in range(tc // SUBLANES):
        rows = slice(gi * SUBLANES, (gi + 1) * SUBLANES)
        blk = a8[rows] * h + u8[rows]
        hs[rows, :] = blk
        h = blk[SUBLANES - 1:SUBLANES, :]
        cblk = lf8[rows] + c
        c_ref[rows, :] = cblk
        c = cblk[SUBLANES - 1:SUBLANES, :]
    hprev[...] = h
    cprev[...] = c

    yr = yr_ref[...]
    gelu = 0.5 * yr * (1.0 + jnp.tanh(0.7978845608028654 * (yr + 0.044715 * (yr * yr * yr))))
    ya_ref[...] = (hs[...] * gelu).astype(BF16)


def _lru(xy, f, cw, cb, wga, wgx, gab, gxb, lam, fb, bsz, seq):
    t = xy.shape[0]
    w = LRU_WIDTH
    tc = TC_LRU
    ns = seq // tc
    row = lambda b, s: (b * ns + s, 0)
    const = lambda b, s: (0, 0)
    return pl.pallas_call(
        functools.partial(_lru_kernel, tc=tc),
        grid=(bsz, ns),
        in_specs=[pl.BlockSpec((tc, w), row),
                  pl.BlockSpec((tc, w), lambda b, s: (b * ns + s, 1)),
                  pl.BlockSpec((tc, LANES), row),
                  pl.BlockSpec((CONV_W, w), const),
                  pl.BlockSpec((1, w), const),
                  pl.BlockSpec((w, w), const),
                  pl.BlockSpec((w, w), const),
                  pl.BlockSpec((1, w), const),
                  pl.BlockSpec((1, w), const),
                  pl.BlockSpec((1, w), const),
                  pl.BlockSpec((1, LANES), const)],
        out_specs=[pl.BlockSpec((tc, w), row), pl.BlockSpec((tc, LANES), row)],
        out_shape=[jax.ShapeDtypeStruct((t, w), BF16), jax.ShapeDtypeStruct((t, LANES), F32)],
        scratch_shapes=[pltpu.VMEM((SUBLANES, w), F32), pltpu.VMEM((1, w), F32),
                        pltpu.VMEM((1, LANES), F32), pltpu.VMEM((tc, w), F32)],
        compiler_params=_cparams("parallel", "arbitrary"),
        name="lru",
    )(xy, xy, f, cw, cb, wga, wgx, gab, gxb, lam, fb)


def _placement_matrices():
    nh, hd = FOX_HEADS, FOX_HEAD_DIM
    w = nh * hd
    pq = np.zeros((nh * LANES, w + LANES), np.float32)
    pk = np.zeros((w + LANES, nh * LANES), np.float32)
    for h in range(nh):
        for j in range(hd):
            pq[h * LANES + j, h * hd + j] = 1.0
            pk[h * hd + j, h * LANES + j] = 1.0
        for part in range(3):
            pq[h * LANES + hd + part, w + part * nh + h] = 1.0
            pk[w + part * nh + h, h * LANES + hd + 3 + part] = -1.0
    return jnp.asarray(pq, BF16), jnp.asarray(pk, BF16)


def _attn_prep_kernel(qkv_ref, c_ref, pq_ref, pk_ref, qt_ref, k_ref, vt_ref):
    hd = FOX_HEAD_DIM
    w = FOX_HEADS * hd
    nh = FOX_HEADS
    lane = lax.broadcasted_iota(jnp.int32, (1, LANES), 1)
    c = jnp.where(lane < nh, c_ref[...] * LOG2E, 0.0)
    hi = c.astype(BF16).astype(F32)
    r1 = c - hi
    mid = r1.astype(BF16).astype(F32)
    lo = (r1 - mid).astype(BF16).astype(F32)
    cs = (hi + pltpu.roll(mid, nh, 1) + pltpu.roll(lo, 2 * nh, 1)).astype(BF16)
    qkv = qkv_ref[...]
    q = (qkv[:, :w].astype(F32) * (hd ** -0.5 * LOG2E)).astype(BF16)
    xq = jnp.concatenate([q, cs], axis=1)
    xk = jnp.concatenate([qkv[:, w:2 * w], cs], axis=1)
    qt = _dot_nt(pq_ref[...], xq)
    feat = lax.broadcasted_iota(jnp.int32, (qt.shape[0], 1), 0) & (LANES - 1)
    qt = qt + jnp.where((feat >= hd + 3) & (feat < hd + 6), 1.0, 0.0)
    for h in range(FOX_HEADS):
        qt_ref[h] = qt[h * LANES:(h + 1) * LANES, :].astype(BF16)
    ka = _dot(xk, pk_ref[...])
    col = lax.broadcasted_iota(jnp.int32, (1, ka.shape[1]), 1) & (LANES - 1)
    k_ref[...] = (ka + jnp.where((col >= hd) & (col < hd + 3), 1.0, 0.0)).astype(BF16)
    vt = qkv[:, 2 * w:].astype(F32).T
    for hp in range(FOX_HEADS // 2):
        vt_ref[hp, 0] = vt[hp * LANES:(hp + 1) * LANES, :].astype(BF16)


def _attn_prep(qkv, c, bsz, seq):
    nh = FOX_HEADS
    tb = min(TQ_ATTN, seq)
    ns = seq // tb
    pq, pk = _placement_matrices()
    const = lambda b, s: (0, 0)
    return pl.pallas_call(
        _attn_prep_kernel,
        grid=(bsz, ns),
        in_specs=[pl.BlockSpec((tb, qkv.shape[1]), lambda b, s: (b * ns + s, 0)),
                  pl.BlockSpec((tb, LANES), lambda b, s: (b * ns + s, 0)),
                  pl.BlockSpec(pq.shape, const), pl.BlockSpec(pk.shape, const)],
        out_specs=[pl.BlockSpec((None, nh, LANES, tb), lambda b, s: (b, 0, 0, s)),
                   pl.BlockSpec((tb, nh * LANES), lambda b, s: (b * ns + s, 0)),
                   pl.BlockSpec((None, nh // 2, 1, LANES, tb), lambda b, s: (b, 0, s, 0, 0))],
        out_shape=[jax.ShapeDtypeStruct((bsz, nh, LANES, seq), BF16),
                   jax.ShapeDtypeStruct((bsz * seq, nh * LANES), BF16),
                   jax.ShapeDtypeStruct((bsz, nh // 2, ns, LANES, tb), BF16)],
        compiler_params=_cparams("parallel", "parallel"),
        name="attn_prep",
    )(qkv, c, pq, pk)


def _attn_kernel(qt_ref, k_ref, vt_ref, wa_ref, wb_ref, wc_ref, o_ref, wa_out, wb_out, wc_out,
                 m_sc, l_sc, acc_sc, s_sc, *, tk):
    wa_out[...] = wa_ref[...].astype(BF16)
    wb_out[...] = wb_ref[...].astype(BF16)
    wc_out[...] = wc_ref[...].astype(BF16)
    qi = pl.program_id(2)
    m_sc[...] = jnp.full_like(m_sc, NEG_BIG)
    l_sc[...] = jnp.zeros_like(l_sc)
    acc_sc[...] = jnp.zeros_like(acc_sc)

    def logits(kb, hh):
        k2 = k_ref[pl.ds(pl.multiple_of(kb * tk, tk), tk), hh * LANES:(hh + 1) * LANES]
        return _dot(k2, qt_ref[hh])

    def update(kb, hh, s, masked):
        if masked:
            key = lax.broadcasted_iota(jnp.int32, s.shape, 0)
            qry = lax.broadcasted_iota(jnp.int32, s.shape, 1)
            s = jnp.where(key <= qry, s, NEG_BIG)
        m_prev = m_sc[hh]
        m_new = jnp.maximum(m_prev, jnp.max(s, axis=0, keepdims=True))
        alpha = jnp.exp2(m_prev - m_new)
        p = jnp.exp2(s - m_new)
        l_sc[hh] = alpha * l_sc[hh] + jnp.sum(p, axis=0, keepdims=True)
        acc_sc[hh] = alpha * acc_sc[hh] + _dot(vt_ref[kb], p.astype(BF16))
        m_sc[hh] = m_new

    for hh in range(2):
        s_sc[hh] = logits(0, hh)

    def body(kb, carry):
        for hh in range(2):
            s = s_sc[hh]
            s_next = logits(kb + 1, hh)
            update(kb, hh, s, False)
            s_sc[hh] = s_next
        return carry

    lax.fori_loop(0, qi, body, 0)
    for hh in range(2):
        update(qi, hh, s_sc[hh], True)
    feat = lax.broadcasted_iota(jnp.int32, (LANES, 1), 0)
    ot = jnp.where(feat < FOX_HEAD_DIM, acc_sc[0] / l_sc[0], acc_sc[1] / l_sc[1])
    o_ref[...] = ot.T.astype(o_ref.dtype)


def _attn(qt, k_aug, vt, weights, layer, bsz, seq):
    nk, tk = vt.shape[2], vt.shape[4]
    tq = tk
    nq = seq // tq
    npair = FOX_HEADS // 2
    steps = bsz * npair * nq
    flat = [w.reshape(-1, w.shape[-1]) for w in weights]
    rows = [w.shape[0] // (weights[0].shape[0] * steps) for w in flat]
    step = lambda b, h, qi: (b * npair + h) * nq + qi
    in_w = [pl.BlockSpec((r, w.shape[1]), lambda b, h, qi: (layer * steps + step(b, h, qi), 0))
            for r, w in zip(rows, flat)]
    out_w = [pl.BlockSpec((r, w.shape[1]), lambda b, h, qi: (step(b, h, qi), 0)) for r, w in zip(rows, flat)]
    outs = pl.pallas_call(
        functools.partial(_attn_kernel, tk=tk),
        grid=(bsz, npair, nq),
        in_specs=[pl.BlockSpec((None, 2, LANES, tq), lambda b, h, qi: (b, h, 0, qi)),
                  pl.BlockSpec((seq, 2 * LANES), lambda b, h, qi: (b, h)),
                  pl.BlockSpec((None, None, nk, LANES, tk), lambda b, h, qi: (b, h, 0, 0, 0))] + in_w,
        out_specs=[pl.BlockSpec((tq, LANES), lambda b, h, qi: (b * nq + qi, h))] + out_w,
        out_shape=[jax.ShapeDtypeStruct((bsz * seq, FOX_HEADS * FOX_HEAD_DIM), BF16)]
        + [jax.ShapeDtypeStruct((r * steps, w.shape[1]), BF16) for r, w in zip(rows, flat)],
        scratch_shapes=[pltpu.VMEM((2, 1, tq), F32), pltpu.VMEM((2, 1, tq), F32),
                        pltpu.VMEM((2, LANES, tq), F32), pltpu.VMEM((2, tk, tq), F32)],
        compiler_params=_cparams("parallel", "parallel", "arbitrary"),
        name="attn",
    )(qt, k_aug, vt, *flat)
    return outs[0], [o.reshape(w.shape[1:]) for o, w in zip(outs[1:], weights)]


def _swiglu(xn, w1_ref, w3_ref, w2_ref, h_sc, fc):
    f = w1_ref.shape[1]
    for c0 in range(0, f, fc):
        a = _dot(xn, w1_ref[:, c0:c0 + fc])
        b = _dot(xn, w3_ref[:, c0:c0 + fc])
        h_sc[:, c0:c0 + fc] = (a * _sigmoid(a) * b).astype(BF16)
    return _dot(h_sc[...], w2_ref[...])


def _ffn_kernel(h_ref, ya_ref, yb_ref, wo_ref, g_ref, w1_ref, w3_ref, w2_ref, o_ref, h_sc):
    half = ya_ref.shape[1]
    h1 = h_ref[...] + _dot(ya_ref[...], wo_ref[:half, :]) + _dot(yb_ref[...], wo_ref[half:, :])
    xn = _rmsnorm(h1, g_ref[...]).astype(BF16)
    o_ref[...] = h1 + _swiglu(xn, w1_ref, w3_ref, w2_ref, h_sc, FC_FFN)


def _ffn(h, ya, yb, wo, g, w1, w3, w2):
    t, d = h.shape
    tm = TM_FFN
    f = w1.shape[1]
    half = ya.shape[1]
    row = lambda i: (i, 0)
    c2 = lambda i: (0, 0)
    return pl.pallas_call(
        _ffn_kernel,
        grid=(t // tm,),
        in_specs=[pl.BlockSpec((tm, d), row), pl.BlockSpec((tm, half), row), pl.BlockSpec((tm, half), row),
                  pl.BlockSpec((d, d), c2), pl.BlockSpec((1, d), c2),
                  pl.BlockSpec((d, f), c2), pl.BlockSpec((d, f), c2), pl.BlockSpec((f, d), c2)],
        out_specs=pl.BlockSpec((tm, d), row),
        out_shape=jax.ShapeDtypeStruct((t, d), F32),
        scratch_shapes=[pltpu.VMEM((tm, f), BF16)],
        compiler_params=_cparams("parallel"),
        name="ffn",
    )(h, ya, yb, wo, g, w1, w3, w2)


def _gla_kernel(q_ref, k_ref, v_ref, g_ref, lr_ref, gw_ref, gb_ref, hn_ref, o_ref, state_t, *, rows):
    @pl.when(pl.program_id(2) == 0)
    def _():
        state_t[...] = jnp.zeros_like(state_t)

    cs = GLA_CHUNK
    log_a = _log_sigmoid(_dot(lr_ref[...].astype(BF16), gw_ref[...]) + gb_ref[...]) * (1.0 / GLA_TAU)
    rowi = lax.broadcasted_iota(jnp.int32, (cs, GLA_DK), 0)
    tri = (lax.broadcasted_iota(jnp.int32, (cs, cs), 0) >= lax.broadcasted_iota(jnp.int32, (cs, cs), 1))
    for ci in range(rows // cs):
        sl = slice(ci * cs, (ci + 1) * cs)
        b = log_a[sl]
        d = 1
        while d < cs:
            b = b + jnp.where(rowi >= d, pltpu.roll(b, d, 0), 0.0)
            d *= 2
        b_last = b[cs - 1:cs, :]
        q = q_ref[sl, :] * (GLA_DK ** -0.5)
        k = k_ref[sl, :]
        v = v_ref[sl, :]
        q_dec = (q * jnp.exp(b)).astype(BF16)
        k_intra = (k * jnp.exp(-b)).astype(BF16)
        k_state = (k * jnp.exp(b_last - b)).astype(BF16)
        scores = jnp.where(tri, _dot_nt(q_dec, k_intra), 0.0)
        st = state_t[...]
        o = _dot(scores.astype(BF16), v) + _dot_nt(q_dec, st.astype(BF16))
        state_t[...] = st * jnp.exp(b_last) + _dot_tn(v, k_state)
        o = o * lax.rsqrt(jnp.mean(o * o, axis=-1, keepdims=True) + EPS) * hn_ref[...]
        g = g_ref[sl, :]
        o_ref[sl, :] = (o * (g * _sigmoid(g))).astype(o_ref.dtype)


def _gla(qk, v, g, lr, gw, gb, hn, bsz, seq):
    t = qk.shape[0]
    rows = min(R_GLA, seq)
    nr = seq // rows
    nh = GLA_HEADS
    return pl.pallas_call(
        functools.partial(_gla_kernel, rows=rows),
        grid=(bsz, nh, nr),
        in_specs=[pl.BlockSpec((rows, GLA_DK), lambda b, h, r: (b * nr + r, h)),
                  pl.BlockSpec((rows, GLA_DK), lambda b, h, r: (b * nr + r, nh + h)),
                  pl.BlockSpec((rows, GLA_DV), lambda b, h, r: (b * nr + r, h)),
                  pl.BlockSpec((rows, GLA_DV), lambda b, h, r: (b * nr + r, h)),
                  pl.BlockSpec((rows, LANES), lambda b, h, r: (b * nr + r, 0)),
                  pl.BlockSpec((LANES, GLA_DK), lambda b, h, r: (0, h)),
                  pl.BlockSpec((1, GLA_DK), lambda b, h, r: (0, h)),
                  pl.BlockSpec((1, GLA_DV), lambda b, h, r: (0, h))],
        out_specs=pl.BlockSpec((rows, GLA_DV), lambda b, h, r: (b * nr + r, h)),
        out_shape=jax.ShapeDtypeStruct((t, nh * GLA_DV), BF16),
        scratch_shapes=[pltpu.VMEM((GLA_DV, GLA_DK), F32)],
        compiler_params=_cparams("parallel", "parallel", "arbitrary"),
        name="gla",
    )(qk, qk, v, g, lr, gw, gb, hn)


def _router_kernel(h_ref, o_ref, wo_ref, g_ref, rw_ref, h1_ref, xn_ref, info_ref):
    h1 = h_ref[...] + _dot(o_ref[...], wo_ref[...])
    h1_ref[...] = h1
    xn = _rmsnorm(h1, g_ref[...])
    _split_lanes(xn_ref, _pack_bf16_pairs(xn))
    rw = rw_ref[...]
    xh = xn.astype(BF16)
    xl = (xn - xh.astype(F32)).astype(BF16)
    rh = rw.astype(BF16)
    rl = (rw - rh.astype(F32)).astype(BF16)
    logits = _dot(xh, rh) + (_dot(xh, rl) + _dot(xl, rh))
    lane = lax.broadcasted_iota(jnp.int32, logits.shape, 1)
    lg = jnp.where(lane < N_EXPERTS, logits, -jnp.inf)
    m1 = jnp.max(lg, axis=-1, keepdims=True)
    i1 = jnp.min(jnp.where(lg == m1, lane, LANES), axis=-1, keepdims=True)
    lg2 = jnp.where(lane == i1, -jnp.inf, lg)
    m2 = jnp.max(lg2, axis=-1, keepdims=True)
    i2 = jnp.min(jnp.where(lg2 == m2, lane, LANES), axis=-1, keepdims=True)
    e = jnp.exp(m2 - m1)
    w1 = 1.0 / (1.0 + e)
    w2 = e * w1
    info = jnp.where(lane == 0, i1.astype(F32),
                     jnp.where(lane == 1, i2.astype(F32),
                               jnp.where(lane == 2, w1, jnp.where(lane == 3, w2, 0.0))))
    info_ref[...] = info


def _router(h, o, wo, g, rw):
    t, d = h.shape
    tm = TM_ROUTER
    row = lambda i: (i, 0)
    c2 = lambda i: (0, 0)
    return pl.pallas_call(
        _router_kernel,
        grid=(t // tm,),
        in_specs=[pl.BlockSpec((tm, d), row), pl.BlockSpec((tm, d), row), pl.BlockSpec((d, d), c2),
                  pl.BlockSpec((1, d), c2), pl.BlockSpec((d, LANES), c2)],
        out_specs=[pl.BlockSpec((tm, d), row), pl.BlockSpec((d // (2 * PIECE), tm, PIECE), lambda i: (0, i, 0)),
                   pl.BlockSpec((tm, LANES), row)],
        out_shape=[jax.ShapeDtypeStruct((t, d), F32), jax.ShapeDtypeStruct((d // (2 * PIECE), t, PIECE), jnp.uint32),
                   jax.ShapeDtypeStruct((t, LANES), F32)],
        compiler_params=_cparams("parallel"),
        name="router",
    )(h, o, wo, g, rw)


def _moe_kernel(te_ref, nu_ref, valid_ref, xs_ref, w1_ref, w3_ref, w2_ref, y_ref, h_sc):
    i = pl.program_id(0)

    @pl.when(i < nu_ref[0])
    def _():
        packed = _join_lanes(xs_ref)
        row = lax.broadcasted_iota(jnp.int32, packed.shape, 0)
        packed = jnp.where(row < valid_ref[i], packed, jnp.uint32(0))
        x = _unpack_bf16_pairs(packed).astype(BF16)
        _split_lanes(y_ref, _pack_bf16_pairs(_swiglu(x, w1_ref, w3_ref, w2_ref, h_sc, FC_MOE)))

    @pl.when(i >= nu_ref[0])
    def _():
        y_ref[...] = jnp.zeros_like(y_ref)


def _moe(tile_expert, n_used, valid, xs, w1, w3, w2):
    pieces, p, width = xs.shape
    tm = TM_MOE
    d, f = w1.shape[1:]
    grid_spec = pltpu.PrefetchScalarGridSpec(
        num_scalar_prefetch=3,
        grid=(p // tm,),
        in_specs=[pl.BlockSpec((pieces, tm, width), lambda i, te, nu, va: (0, i, 0)),
                  pl.BlockSpec((None, d, f), lambda i, te, nu, va: (te[i], 0, 0)),
                  pl.BlockSpec((None, d, f), lambda i, te, nu, va: (te[i], 0, 0)),
                  pl.BlockSpec((None, f, d), lambda i, te, nu, va: (te[i], 0, 0))],
        out_specs=pl.BlockSpec((pieces, tm, width), lambda i, te, nu, va: (0, i, 0)),
        scratch_shapes=[pltpu.VMEM((tm, f), BF16)],
    )
    return pl.pallas_call(
        _moe_kernel,
        grid_spec=grid_spec,
        out_shape=jax.ShapeDtypeStruct((pieces, p, width), jnp.uint32),
        compiler_params=pltpu.CompilerParams(dimension_semantics=("arbitrary",), vmem_limit_bytes=VMEM_LIMIT_MOE),
        name="moe",
    )(tile_expert, n_used, valid, xs, w1, w3, w2)


def _combine_kernel(h1_ref, y1_ref, y2_ref, info_ref, gf_ref, o_ref, *, final):
    info = info_ref[...]
    y1 = _unpack_bf16_pairs(_join_lanes(y1_ref))
    y2 = _unpack_bf16_pairs(_join_lanes(y2_ref))
    out = h1_ref[...] + info[:, 2:3] * y1 + info[:, 3:4] * y2
    if final:
        out = _rmsnorm(out, gf_ref[...])
    o_ref[...] = out


def _combine(h1, yg, info, gf, final):
    t, d = h1.shape
    tm = TM_COMBINE
    nt = t // tm
    return pl.pallas_call(
        functools.partial(_combine_kernel, final=final),
        grid=(nt,),
        in_specs=[pl.BlockSpec((tm, d), lambda i: (i, 0)),
                  pl.BlockSpec((d // (2 * PIECE), tm, PIECE), lambda i: (0, i, 0)),
                  pl.BlockSpec((d // (2 * PIECE), tm, PIECE), lambda i: (0, nt + i, 0)),
                  pl.BlockSpec((tm, LANES), lambda i: (i, 0)),
                  pl.BlockSpec((1, d), lambda i: (0, 0))],
        out_specs=pl.BlockSpec((tm, d), lambda i: (i, 0)),
        out_shape=jax.ShapeDtypeStruct((t, d), F32),
        compiler_params=_cparams("parallel"),
        name="combine",
    )(h1, yg, yg, info, gf)


def _sc_gather(x, idx):
    pieces, rows, lanes = x.shape
    idx = (jnp.arange(pieces, dtype=jnp.int32)[:, None] * rows + idx[None, :]).reshape(-1)
    out = _sc_gather_pieces(x.reshape(pieces * rows, lanes), idx)
    return out.reshape(pieces, -1, lanes)


def _sc_scatter(x, pos, out_rows):
    pieces, t, width = x.shape
    idx = (jnp.arange(pieces, dtype=jnp.int32)[:, None] * out_rows + pos[None, :]).reshape(-1)
    out = _sc_scatter_pieces(x.reshape(pieces * t, width), idx, pos.shape[0], pieces * out_rows)
    return out.reshape(pieces, out_rows, width)


def _sc_scatter_pieces(x, idx, slots, out_rows):
    n = idx.shape[0]
    d = x.shape[1]
    win = SC_GATHER_WINDOW
    pieces = n // slots
    nb_slots = slots // win
    nb_rows = x.shape[0] // pieces // win
    mesh = plsc.VectorSubcoreMesh(core_axis_name="c", subcore_axis_name="s")

    @functools.partial(pl.kernel, out_type=jax.ShapeDtypeStruct((out_rows, d), x.dtype), mesh=mesh)
    def scatter_kernel(x_hbm, i_hbm, o_hbm):
        def body(x_vmem, i_vmem):
            pltpu.sync_copy(x_vmem, o_hbm.at[i_vmem.at[0]])

        pltpu.emit_pipeline(
            body,
            grid=(n // win,),
            in_specs=[pl.BlockSpec((win, d), lambda i: ((i // nb_slots) * nb_rows + (i % nb_slots) % nb_rows, 0)),
                      pl.BlockSpec((1, win), lambda i: (0, i))],
            out_specs=[],
            core_axis_name=("c", "s"),
            dimension_semantics=(pltpu.PARALLEL,),
        )(x_hbm, i_hbm)

    return scatter_kernel(x, idx.reshape(1, n))


def _bf16_bits(a):
    bits = lax.bitcast_convert_type(a, jnp.uint32)
    return (bits + (jnp.uint32(0x7FFF) + ((bits >> 16) & jnp.uint32(1)))) >> 16


def _pack_bf16_pairs(x):
    w = x.shape[1] // 2
    return _bf16_bits(x[:, :w]) | (_bf16_bits(x[:, w:]) << 16)


def _unpack_bf16_pairs(p):
    lo = lax.bitcast_convert_type(p << 16, F32)
    hi = lax.bitcast_convert_type(p & jnp.uint32(0xFFFF0000), F32)
    return jnp.concatenate([lo, hi], axis=1)


def _split_lanes(ref, x):
    width = ref.shape[2]
    for k in range(ref.shape[0]):
        ref[k] = x[:, k * width:(k + 1) * width].astype(ref.dtype)


def _join_lanes(ref):
    return jnp.concatenate([ref[k] for k in range(ref.shape[0])], axis=1)


def _sc_gather_pieces(x, idx):
    n = idx.shape[0]
    d = x.shape[1]
    win = SC_GATHER_WINDOW
    mesh = plsc.VectorSubcoreMesh(core_axis_name="c", subcore_axis_name="s")

    @functools.partial(pl.kernel, out_type=jax.ShapeDtypeStruct((n, d), x.dtype), mesh=mesh)
    def gather_kernel(x_hbm, i_hbm, o_hbm):
        def body(i_vmem, o_vmem):
            pltpu.sync_copy(x_hbm.at[i_vmem.at[0]], o_vmem)

        pltpu.emit_pipeline(
            body,
            grid=(n // win,),
            in_specs=[pl.BlockSpec((1, win), lambda i: (0, i))],
            out_specs=[pl.BlockSpec((win, d), lambda i: (i, 0))],
            core_axis_name=("c", "s"),
            dimension_semantics=(pltpu.PARALLEL,),
        )(i_hbm, o_hbm)

    return gather_kernel(x, idx.reshape(1, n))


def _routing_plan(info, t):
    tm = TM_MOE
    n_tiles = (2 * t) // tm + N_EXPERTS
    top = jnp.concatenate([info[:, 0], info[:, 1]]).astype(jnp.int32)
    onehot = (top[:, None] == jnp.arange(N_EXPERTS, dtype=jnp.int32)[None, :]).astype(jnp.int32)
    csum = jnp.cumsum(onehot, axis=0)
    rank = jnp.sum((csum - onehot) * onehot, axis=1)
    counts = csum[-1]
    ntile = (counts + tm - 1) // tm
    tile_end = jnp.cumsum(ntile)
    tile_start = tile_end - ntile
    pos = tile_start[top] * tm + rank
    n_used = tile_end[-1]
    tile_ids = jnp.arange(n_tiles, dtype=jnp.int32)
    te = jnp.sum((tile_ids[:, None] >= tile_end[None, :]).astype(jnp.int32), axis=1)
    last = jnp.sum((n_used - 1 >= tile_end).astype(jnp.int32))
    te = jnp.where(tile_ids < n_used, te, last).astype(jnp.int32)
    valid = jnp.clip(counts[te] - (tile_ids - tile_start[te]) * tm, 0, tm)
    valid = jnp.where(tile_ids < n_used, valid, 0).astype(jnp.int32)
    return te, n_used.reshape(1).astype(jnp.int32), valid, pos, n_tiles * tm


def _pad_cols(w, n):
    return jnp.pad(w, ((0, 0), (0, n - w.shape[1])))


def _block_diag(w):
    nb, bs, _ = w.shape
    eye = jnp.eye(nb, dtype=w.dtype)
    return jnp.einsum('nij,nm->nimj', w, eye).reshape(nb * bs, nb * bs)


def kernel(x, norm_mix, norm_ffn, norm_final, ev_w_in, ev_conv_w, ev_conv_b, ev_ga_w, ev_ga_b, ev_gx_w, ev_gx_b, ev_lambda, ev_f_b, ev_w_out, ev_ffn_w1, ev_ffn_w3, ev_ffn_w2, od_w_in, od_gate_w2, od_gate_b, od_head_norm, od_w_out, od_router, od_exp_w1, od_exp_w3, od_exp_w2):
    bsz, seq, d = x.shape
    t = bsz * seq
    depth = norm_mix.shape[0]
    h = x.reshape(t, d)
    n_qkv = 3 * FOX_HEADS * FOX_HEAD_DIM
    n_gla_k = 2 * GLA_HEADS * GLA_DK
    n_gla_v = GLA_HEADS * GLA_DV
    n_even_main = 2 * LRU_WIDTH + n_qkv
    n_odd_main = n_gla_k + 2 * n_gla_v
    row = lambda v: v.reshape(1, -1)
    expert_w = None

    for layer in range(depth):
        j = layer // 2
        g_mix = row(norm_mix[layer])
        g_ffn = row(norm_ffn[layer])
        if layer % 2 == 0:
            w_in = ev_w_in[j]
            w = jnp.concatenate([w_in[:, :n_even_main], _pad_cols(w_in[:, n_even_main:], LANES)], axis=1).astype(BF16)
            xy, qkv, f = _proj(h, g_mix, w, [(0, 2 * LRU_WIDTH, F32), (2 * LRU_WIDTH, n_qkv, BF16),
                                             (n_even_main, LANES, F32)])
            fb = _pad_cols(row(ev_f_b[j]), LANES)
            ya, c = _lru(xy, f, ev_conv_w[j], row(ev_conv_b[j]),
                         _block_diag(ev_ga_w[j]).astype(BF16), _block_diag(ev_gx_w[j]).astype(BF16),
                         row(ev_ga_b[j]), row(ev_gx_b[j]), row(ev_lambda[j]), fb, bsz, seq)
            nxt = min(j, od_exp_w1.shape[0] - 1)
            yb, expert_w = _attn(*_attn_prep(qkv, c, bsz, seq), (od_exp_w1, od_exp_w3, od_exp_w2), nxt, bsz, seq)
            h = _ffn(h, ya, yb, ev_w_out[j].astype(BF16), g_ffn,
                     ev_ffn_w1[j].astype(BF16), ev_ffn_w3[j].astype(BF16), ev_ffn_w2[j].astype(BF16))
        else:
            w_in = od_w_in[j]
            w = jnp.concatenate([w_in[:, :n_odd_main], _pad_cols(w_in[:, n_odd_main:], LANES)], axis=1).astype(BF16)
            qk, v, g, lr = _proj(h, g_mix, w, [(0, n_gla_k, F32), (n_gla_k, n_gla_v, BF16),
                                               (n_gla_k + n_gla_v, n_gla_v, F32), (n_odd_main, LANES, F32)])
            gw = jnp.pad(od_gate_w2[j], ((0, LANES - GLA_RANK), (0, 0))).astype(BF16)
            o = _gla(qk, v, g, lr, gw, row(od_gate_b[j]), row(od_head_norm[j]), bsz, seq)
            rw = _pad_cols(od_router[j], LANES)
            h1, xn, info = _router(h, o, od_w_out[j].astype(BF16), g_ffn, rw)
            te, n_used, valid, pos, p_rows = _routing_plan(info, t)
            xs = _sc_scatter(xn, pos, p_rows)
            y = _moe(te, n_used, valid, xs, *expert_w)
            yg = _sc_gather(y, pos)
            final = layer == depth - 1
            h = _combine(h1, yg, info, row(norm_final), final)
    if depth % 2 == 1:
        raise NotImplementedError("final norm is fused into the last odd layer")
    return h.reshape(bsz, seq, d)
```

```python
import functools

import jax
import jax.numpy as jnp
import numpy as np
from jax import lax
from jax.experimental import pallas as pl
from jax.experimental.pallas import tpu as pltpu
from jax.experimental.pallas import tpu_sc as plsc

F32 = jnp.float32
BF16 = jnp.bfloat16

EPS = 1e-6
LRU_WIDTH = 512
LRU_BLOCKS = 8
CONV_W = 4
LRU_C = 8.0
FOX_HEADS = 8
FOX_HEAD_DIM = 64
GLA_HEADS = 4
GLA_DK = 128
GLA_DV = 256
GLA_RANK = 16
GLA_TAU = 16.0
GLA_CHUNK = 64
N_EXPERTS = 8
LANES = 128
SUBLANES = 8
LOG2E = 1.4426950408889634
NEG_BIG = -1e30
VMEM_LIMIT = 56 * 1024 * 1024
VMEM_LIMIT_MOE = 60 * 1024 * 1024

TM_PROJ = 1024
TC_LRU = 512
TQ_ATTN = 512
TM_FFN = 512
FC_FFN = 256
R_GLA = 2048
TM_ROUTER = 1024
TM_MOE = 256
FC_MOE = 512
TM_COMBINE = 1024
SC_GATHER_WINDOW = 128
PIECE = 256


def _cparams(*sem):
    return pltpu.CompilerParams(dimension_semantics=sem, vmem_limit_bytes=VMEM_LIMIT)


def _rmsnorm(x, g):
    return x * lax.rsqrt(jnp.mean(x * x, axis=-1, keepdims=True) + EPS) * g


def _sigmoid(x):
    return 1.0 / (1.0 + jnp.exp(-x))


def _log_sigmoid(z):
    return jnp.minimum(z, 0.0) - jnp.log1p(jnp.exp(-jnp.abs(z)))


def _dot(a, b):
    return jnp.dot(a, b, preferred_element_type=F32)


def _dot_nt(a, b):
    return lax.dot_general(a, b, (((1,), (1,)), ((), ())), preferred_element_type=F32)


def _dot_tn(a, b):
    return lax.dot_general(a, b, (((0,), (0,)), ((), ())), preferred_element_type=F32)


def _proj_kernel(h_ref, g_ref, w_ref, *refs, splits, n_cast):
    cast_in, out_refs, cast_out = refs[:n_cast], refs[n_cast:n_cast + len(splits)], refs[n_cast + len(splits):]
    for src, dst in zip(cast_in, cast_out):
        dst[...] = src[...].astype(BF16)
    xn = _rmsnorm(h_ref[...], g_ref[...]).astype(BF16)
    for o_ref, (start, width) in zip(out_refs, splits):
        o_ref[...] = _dot(xn, w_ref[:, start:start + width]).astype(o_ref.dtype)


def _proj(h, g, w, outs, cast=(), layer=0):
    t, d = h.shape
    n = w.shape[1]
    tm = TM_PROJ
    steps = t // tm
    splits = tuple((s, wd) for s, wd, _ in outs)
    flat = [c.reshape(-1, c.shape[-1]) for c in cast]
    rows = [c.shape[1] // steps for c in cast]
    in_c = [pl.BlockSpec((r, f.shape[1]), lambda i: (layer * steps + i, 0)) for r, f in zip(rows, flat)]
    out_c = [pl.BlockSpec((r, f.shape[1]), lambda i: (i, 0)) for r, f in zip(rows, flat)]
    res = pl.pallas_call(
        functools.partial(_proj_kernel, splits=splits, n_cast=len(cast)),
        grid=(steps,),
        in_specs=[pl.BlockSpec((tm, d), lambda i: (i, 0)),
                  pl.BlockSpec((1, d), lambda i: (0, 0)),
                  pl.BlockSpec((d, n), lambda i: (0, 0))] + in_c,
        out_specs=[pl.BlockSpec((tm, wd), lambda i: (i, 0)) for _, wd, _ in outs] + out_c,
        out_shape=[jax.ShapeDtypeStruct((t, wd), dt) for _, wd, dt in outs]
        + [jax.ShapeDtypeStruct(c.shape[1:], BF16) for c in cast],
        compiler_params=_cparams("parallel"),
        name="proj",
    )(h, g, w, *flat)
    return res[:len(outs)], res[len(outs):]


def _scan8(a, u):
    row = lax.broadcasted_iota(jnp.int32, u.shape, 0) & (SUBLANES - 1)
    for d in (1, 2, 4):
        m = row >= d
        u_sh = pltpu.roll(u, d, 0)
        if a is None:
            u = jnp.where(m, u + u_sh, u)
        else:
            a_sh = pltpu.roll(a, d, 0)
            u = jnp.where(m, a * u_sh + u, u)
            a = jnp.where(m, a * a_sh, a)
    return a, u


def _lru_kernel(xr_ref, yr_ref, f_ref, cw_ref, cb_ref, wga_ref, wgx_ref, gab_ref, gxb_ref,
                lam_ref, fb_ref, ya_ref, c_ref, xprev, hprev, cprev, hs, *, tc):
    @pl.when(pl.program_id(1) == 0)
    def _():
        xprev[...] = jnp.zeros_like(xprev)
        hprev[...] = jnp.zeros_like(hprev)
        cprev[...] = jnp.zeros_like(cprev)

    x = xr_ref[...]
    p8 = xprev[...]
    row8 = lax.broadcasted_iota(jnp.int32, p8.shape, 0)
    xc = cb_ref[...] + cw_ref[CONV_W - 1:CONV_W, :] * x
    for j in range(CONV_W - 1):
        sh = CONV_W - 1 - j
        xs = pltpu.roll(x, sh, 0)
        head = jnp.where(row8 < sh, pltpu.roll(p8, sh, 0), xs[:SUBLANES])
        xs = jnp.concatenate([head, xs[SUBLANES:]], axis=0)
        xc = xc + cw_ref[j:j + 1, :] * xs
    xprev[...] = x[tc - SUBLANES:, :]

    xb = xc.astype(BF16)
    r = _sigmoid(_dot(xb, wga_ref[...]) + gab_ref[...])
    i = _sigmoid(_dot(xb, wgx_ref[...]) + gxb_ref[...])
    nl = -lam_ref[...]
    softplus_nl = jnp.maximum(nl, 0.0) + jnp.log1p(jnp.exp(-jnp.abs(nl)))
    log_a = (-LRU_C) * r * softplus_nl
    a = jnp.exp(log_a)
    u = jnp.sqrt(-jnp.tanh(log_a) * (1.0 + a * a)) * (i * xc)

    a8, u8 = _scan8(a, u)
    _, lf8 = _scan8(None, _log_sigmoid(f_ref[...] + fb_ref[...]))

    h = hprev[...]
    c = cprev[...]
    for gi in range(tc // SUBLANES):
        rows = slice(gi * SUBLANES, (gi + 1) * SUBLANES)
        blk = a8[rows] * h + u8[rows]
        hs[rows, :] = blk
        h = blk[SUBLANES - 1:SUBLANES, :]
        cblk = lf8[rows] + c
        c_ref[rows, :] = cblk
        c = cblk[SUBLANES - 1:SUBLANES, :]
    hprev[...] = h
    cprev[...] = c

    yr = yr_ref[...]
    gelu = 0.5 * yr * (1.0 + jnp.tanh(0.7978845608028654 * (yr + 0.044715 * (yr * yr * yr))))
    ya_ref[...] = (hs[...] * gelu).astype(BF16)


def _lru(xy, f, cw, cb, wga, wgx, gab, gxb, lam, fb, bsz, seq):
    t = xy.shape[0]
    w = LRU_WIDTH
    tc = TC_LRU
    ns = seq // tc
    row = lambda b, s: (b * ns + s, 0)
    const = lambda b, s: (0, 0)
    return pl.pallas_call(
        functools.partial(_lru_kernel, tc=tc),
        grid=(bsz, ns),
        in_specs=[pl.BlockSpec((tc, w), row),
                  pl.BlockSpec((tc, w), lambda b, s: (b * ns + s, 1)),
                  pl.BlockSpec((tc, LANES), row),
                  pl.BlockSpec((CONV_W, w), const),
                  pl.BlockSpec((1, w), const),
                  pl.BlockSpec((w, w), const),
                  pl.BlockSpec((w, w), const),
                  pl.BlockSpec((1, w), const),
                  pl.BlockSpec((1, w), const),
                  pl.BlockSpec((1, w), const),
                  pl.BlockSpec((1, LANES), const)],
        out_specs=[pl.BlockSpec((tc, w), row), pl.BlockSpec((tc, LANES), row)],
        out_shape=[jax.ShapeDtypeStruct((t, w), BF16), jax.ShapeDtypeStruct((t, LANES), F32)],
        scratch_shapes=[pltpu.VMEM((SUBLANES, w), F32), pltpu.VMEM((1, w), F32),
                        pltpu.VMEM((1, LANES), F32), pltpu.VMEM((tc, w), F32)],
        compiler_params=_cparams("parallel", "arbitrary"),
        name="lru",
    )(xy, xy, f, cw, cb, wga, wgx, gab, gxb, lam, fb)


def _placement_matrices():
    nh, hd = FOX_HEADS, FOX_HEAD_DIM
    w = nh * hd
    pq = np.zeros((nh * LANES, w + LANES), np.float32)
    pk = np.zeros((w + LANES, nh * LANES), np.float32)
    for h in range(nh):
        for j in range(hd):
            pq[h * LANES + j, h * hd + j] = 1.0
            pk[h * hd + j, h * LANES + j] = 1.0
        for part in range(3):
            pq[h * LANES + hd + part, w + part * nh + h] = 1.0
            pk[w + part * nh + h, h * LANES + hd + 3 + part] = -1.0
    return jnp.asarray(pq, BF16), jnp.asarray(pk, BF16)


def _attn_prep_kernel(qkv_ref, c_ref, pq_ref, pk_ref, qt_ref, k_ref, vt_ref):
    hd = FOX_HEAD_DIM
    w = FOX_HEADS * hd
    nh = FOX_HEADS
    lane = lax.broadcasted_iota(jnp.int32, (1, LANES), 1)
    c = jnp.where(lane < nh, c_ref[...] * LOG2E, 0.0)
    hi = c.astype(BF16).astype(F32)
    r1 = c - hi
    mid = r1.astype(BF16).astype(F32)
    lo = (r1 - mid).astype(BF16).astype(F32)
    cs = (hi + pltpu.roll(mid, nh, 1) + pltpu.roll(lo, 2 * nh, 1)).astype(BF16)
    qkv = qkv_ref[...]
    q = (qkv[:, :w].astype(F32) * (hd ** -0.5 * LOG2E)).astype(BF16)
    xq = jnp.concatenate([q, cs], axis=1)
    xk = jnp.concatenate([qkv[:, w:2 * w], cs], axis=1)
    qt = _dot_nt(pq_ref[...], xq)
    feat = lax.broadcasted_iota(jnp.int32, (qt.shape[0], 1), 0) & (LANES - 1)
    qt = qt + jnp.where((feat >= hd + 3) & (feat < hd + 6), 1.0, 0.0)
    for h in range(FOX_HEADS):
        qt_ref[h] = qt[h * LANES:(h + 1) * LANES, :].astype(BF16)
    ka = _dot(xk, pk_ref[...])
    col = lax.broadcasted_iota(jnp.int32, (1, ka.shape[1]), 1) & (LANES - 1)
    k_ref[...] = (ka + jnp.where((col >= hd) & (col < hd + 3), 1.0, 0.0)).astype(BF16)
    vt = qkv[:, 2 * w:].astype(F32).T
    for hp in range(FOX_HEADS // 2):
        vt_ref[hp, 0] = vt[hp * LANES:(hp + 1) * LANES, :].astype(BF16)


def _attn_prep(qkv, c, bsz, seq):
    nh = FOX_HEADS
    tb = min(TQ_ATTN, seq)
    ns = seq // tb
    pq, pk = _placement_matrices()
    const = lambda b, s: (0, 0)
    return pl.pallas_call(
        _attn_prep_kernel,
        grid=(bsz, ns),
        in_specs=[pl.BlockSpec((tb, qkv.shape[1]), lambda b, s: (b * ns + s, 0)),
                  pl.BlockSpec((tb, LANES), lambda b, s: (b * ns + s, 0)),
                  pl.BlockSpec(pq.shape, const), pl.BlockSpec(pk.shape, const)],
        out_specs=[pl.BlockSpec((None, nh, LANES, tb), lambda b, s: (b, 0, 0, s)),
                   pl.BlockSpec((tb, nh * LANES), lambda b, s: (b * ns + s, 0)),
                   pl.BlockSpec((None, nh // 2, 1, LANES, tb), lambda b, s: (b, 0, s, 0, 0))],
        out_shape=[jax.ShapeDtypeStruct((bsz, nh, LANES, seq), BF16),
                   jax.ShapeDtypeStruct((bsz * seq, nh * LANES), BF16),
                   jax.ShapeDtypeStruct((bsz, nh // 2, ns, LANES, tb), BF16)],
        compiler_params=_cparams("parallel", "parallel"),
        name="attn_prep",
    )(qkv, c, pq, pk)


def _attn_kernel(qt_ref, k_ref, vt_ref, wa_ref, wb_ref, wc_ref, o_ref, wa_out, wb_out, wc_out,
                 m_sc, l_sc, acc_sc, s_sc, *, tk):
    wa_out[...] = wa_ref[...].astype(BF16)
    wb_out[...] = wb_ref[...].astype(BF16)
    wc_out[...] = wc_ref[...].astype(BF16)
    qi = pl.program_id(2)
    m_sc[...] = jnp.full_like(m_sc, NEG_BIG)
    l_sc[...] = jnp.zeros_like(l_sc)
    acc_sc[...] = jnp.zeros_like(acc_sc)

    def logits(kb, hh):
        k2 = k_ref[pl.ds(pl.multiple_of(kb * tk, tk), tk), hh * LANES:(hh + 1) * LANES]
        return _dot(k2, qt_ref[hh])

    def update(kb, hh, s, masked):
        if masked:
            key = lax.broadcasted_iota(jnp.int32, s.shape, 0)
            qry = lax.broadcasted_iota(jnp.int32, s.shape, 1)
            s = jnp.where(key <= qry, s, NEG_BIG)
        m_prev = m_sc[hh]
        m_new = jnp.maximum(m_prev, jnp.max(s, axis=0, keepdims=True))
        alpha = jnp.exp2(m_prev - m_new)
        p = jnp.exp2(s - m_new)
        l_sc[hh] = alpha * l_sc[hh] + jnp.sum(p, axis=0, keepdims=True)
        acc_sc[hh] = alpha * acc_sc[hh] + _dot(vt_ref[kb], p.astype(BF16))
        m_sc[hh] = m_new

    for hh in range(2):
        s_sc[hh] = logits(0, hh)

    def body(kb, carry):
        for hh in range(2):
            s = s_sc[hh]
            s_next = logits(kb + 1, hh)
            update(kb, hh, s, False)
            s_sc[hh] = s_next
        return carry

    lax.fori_loop(0, qi, body, 0)
    for hh in range(2):
        update(qi, hh, s_sc[hh], True)
    feat = lax.broadcasted_iota(jnp.int32, (LANES, 1), 0)
    ot = jnp.where(feat < FOX_HEAD_DIM, acc_sc[0] / l_sc[0], acc_sc[1] / l_sc[1])
    o_ref[...] = ot.T.astype(o_ref.dtype)


def _attn(qt, k_aug, vt, weights, layer, bsz, seq):
    nk, tk = vt.shape[2], vt.shape[4]
    tq = tk
    nq = seq // tq
    npair = FOX_HEADS // 2
    steps = bsz * npair * nq
    flat = [w.reshape(-1, w.shape[-1]) for w in weights]
    rows = [w.shape[0] // (weights[0].shape[0] * steps) for w in flat]
    step = lambda b, h, qi: (b * npair + h) * nq + qi
    in_w = [pl.BlockSpec((r, w.shape[1]), lambda b, h, qi: (layer * steps + step(b, h, qi), 0))
            for r, w in zip(rows, flat)]
    out_w = [pl.BlockSpec((r, w.shape[1]), lambda b, h, qi: (step(b, h, qi), 0)) for r, w in zip(rows, flat)]
    outs = pl.pallas_call(
        functools.partial(_attn_kernel, tk=tk),
        grid=(bsz, npair, nq),
        in_specs=[pl.BlockSpec((None, 2, LANES, tq), lambda b, h, qi: (b, h, 0, qi)),
                  pl.BlockSpec((seq, 2 * LANES), lambda b, h, qi: (b, h)),
                  pl.BlockSpec((None, None, nk, LANES, tk), lambda b, h, qi: (b, h, 0, 0, 0))] + in_w,
        out_specs=[pl.BlockSpec((tq, LANES), lambda b, h, qi: (b * nq + qi, h))] + out_w,
        out_shape=[jax.ShapeDtypeStruct((bsz * seq, FOX_HEADS * FOX_HEAD_DIM), BF16)]
        + [jax.ShapeDtypeStruct((r * steps, w.shape[1]), BF16) for r, w in zip(rows, flat)],
        scratch_shapes=[pltpu.VMEM((2, 1, tq), F32), pltpu.VMEM((2, 1, tq), F32),
                        pltpu.VMEM((2, LANES, tq), F32), pltpu.VMEM((2, tk, tq), F32)],
        compiler_params=_cparams("parallel", "parallel", "arbitrary"),
        name="attn",
    )(qt, k_aug, vt, *flat)
    return outs[0], [o.reshape(w.shape[1:]) for o, w in zip(outs[1:], weights)]


def _swiglu(xn, w1_ref, w3_ref, w2_ref, h_sc, fc):
    f = w1_ref.shape[1]
    for c0 in range(0, f, fc):
        a = _dot(xn, w1_ref[:, c0:c0 + fc])
        b = _dot(xn, w3_ref[:, c0:c0 + fc])
        h_sc[:, c0:c0 + fc] = (a * _sigmoid(a) * b).astype(BF16)
    return _dot(h_sc[...], w2_ref[...])


def _ffn_kernel(h_ref, ya_ref, yb_ref, wo_ref, g_ref, w1_ref, w3_ref, w2_ref, o_ref, h_sc):
    half = ya_ref.shape[1]
    h1 = h_ref[...] + _dot(ya_ref[...], wo_ref[:half, :]) + _dot(yb_ref[...], wo_ref[half:, :])
    xn = _rmsnorm(h1, g_ref[...]).astype(BF16)
    o_ref[...] = h1 + _swiglu(xn, w1_ref, w3_ref, w2_ref, h_sc, FC_FFN)


def _ffn(h, ya, yb, wo, g, w1, w3, w2):
    t, d = h.shape
    tm = TM_FFN
    f = w1.shape[1]
    half = ya.shape[1]
    row = lambda i: (i, 0)
    c2 = lambda i: (0, 0)
    return pl.pallas_call(
        _ffn_kernel,
        grid=(t // tm,),
        in_specs=[pl.BlockSpec((tm, d), row), pl.BlockSpec((tm, half), row), pl.BlockSpec((tm, half), row),
                  pl.BlockSpec((d, d), c2), pl.BlockSpec((1, d), c2),
                  pl.BlockSpec((d, f), c2), pl.BlockSpec((d, f), c2), pl.BlockSpec((f, d), c2)],
        out_specs=pl.BlockSpec((tm, d), row),
        out_shape=jax.ShapeDtypeStruct((t, d), F32),
        scratch_shapes=[pltpu.VMEM((tm, f), BF16)],
        compiler_params=_cparams("parallel"),
        name="ffn",
    )(h, ya, yb, wo, g, w1, w3, w2)


def _gla_kernel(q_ref, k_ref, v_ref, g_ref, lr_ref, gw_ref, gb_ref, hn_ref, o_ref, state_t, *, rows):
    @pl.when(pl.program_id(2) == 0)
    def _():
        state_t[...] = jnp.zeros_like(state_t)

    cs = GLA_CHUNK
    log_a = _log_sigmoid(_dot(lr_ref[...].astype(BF16), gw_ref[...]) + gb_ref[...]) * (1.0 / GLA_TAU)
    rowi = lax.broadcasted_iota(jnp.int32, (cs, GLA_DK), 0)
    tri = (lax.broadcasted_iota(jnp.int32, (cs, cs), 0) >= lax.broadcasted_iota(jnp.int32, (cs, cs), 1))
    for ci in range(rows // cs):
        sl = slice(ci * cs, (ci + 1) * cs)
        b = log_a[sl]
        d = 1
        while d < cs:
            b = b + jnp.where(rowi >= d, pltpu.roll(b, d, 0), 0.0)
            d *= 2
        b_last = b[cs - 1:cs, :]
        q = q_ref[sl, :] * (GLA_DK ** -0.5)
        k = k_ref[sl, :]
        v = v_ref[sl, :]
        q_dec = (q * jnp.exp(b)).astype(BF16)
        k_intra = (k * jnp.exp(-b)).astype(BF16)
        k_state = (k * jnp.exp(b_last - b)).astype(BF16)
        scores = jnp.where(tri, _dot_nt(q_dec, k_intra), 0.0)
        st = state_t[...]
        o = _dot(scores.astype(BF16), v) + _dot_nt(q_dec, st.astype(BF16))
        state_t[...] = st * jnp.exp(b_last) + _dot_tn(v, k_state)
        o = o * lax.rsqrt(jnp.mean(o * o, axis=-1, keepdims=True) + EPS) * hn_ref[...]
        g = g_ref[sl, :]
        o_ref[sl, :] = (o * (g * _sigmoid(g))).astype(o_ref.dtype)


def _gla(qk, v, g, lr, gw, gb, hn, bsz, seq):
    t = qk.shape[0]
    rows = min(R_GLA, seq)
    nr = seq // rows
    nh = GLA_HEADS
    return pl.pallas_call(
        functools.partial(_gla_kernel, rows=rows),
        grid=(bsz, nh, nr),
        in_specs=[pl.BlockSpec((rows, GLA_DK), lambda b, h, r: (b * nr + r, h)),
                  pl.BlockSpec((rows, GLA_DK), lambda b, h, r: (b * nr + r, nh + h)),
                  pl.BlockSpec((rows, GLA_DV), lambda b, h, r: (b * nr + r, h)),
                  pl.BlockSpec((rows, GLA_DV), lambda b, h, r: (b * nr + r, h)),
                  pl.BlockSpec((rows, LANES), lambda b, h, r: (b * nr + r, 0)),
                  pl.BlockSpec((LANES, GLA_DK), lambda b, h, r: (0, h)),
                  pl.BlockSpec((1, GLA_DK), lambda b, h, r: (0, h)),
                  pl.BlockSpec((1, GLA_DV), lambda b, h, r: (0, h))],
        out_specs=pl.BlockSpec((rows, GLA_DV), lambda b, h, r: (b * nr + r, h)),
        out_shape=jax.ShapeDtypeStruct((t, nh * GLA_DV), BF16),
        scratch_shapes=[pltpu.VMEM((GLA_DV, GLA_DK), F32)],
        compiler_params=_cparams("parallel", "parallel", "arbitrary"),
        name="gla",
    )(qk, qk, v, g, lr, gw, gb, hn)


def _router_kernel(h_ref, o_ref, wo_ref, g_ref, rw_ref, h1_ref, xn_ref, info_ref):
    h1 = h_ref[...] + _dot(o_ref[...], wo_ref[...])
    h1_ref[...] = h1
    xn = _rmsnorm(h1, g_ref[...])
    _split_lanes(xn_ref, _pack_bf16_pairs(xn))
    rw = rw_ref[...]
    xh = xn.astype(BF16)
    xl = (xn - xh.astype(F32)).astype(BF16)
    rh = rw.astype(BF16)
    rl = (rw - rh.astype(F32)).astype(BF16)
    logits = _dot(xh, rh) + (_dot(xh, rl) + _dot(xl, rh))
    lane = lax.broadcasted_iota(jnp.int32, logits.shape, 1)
    lg = jnp.where(lane < N_EXPERTS, logits, -jnp.inf)
    m1 = jnp.max(lg, axis=-1, keepdims=True)
    i1 = jnp.min(jnp.where(lg == m1, lane, LANES), axis=-1, keepdims=True)
    lg2 = jnp.where(lane == i1, -jnp.inf, lg)
    m2 = jnp.max(lg2, axis=-1, keepdims=True)
    i2 = jnp.min(jnp.where(lg2 == m2, lane, LANES), axis=-1, keepdims=True)
    e = jnp.exp(m2 - m1)
    w1 = 1.0 / (1.0 + e)
    w2 = e * w1
    info = jnp.where(lane == 0, i1.astype(F32),
                     jnp.where(lane == 1, i2.astype(F32),
                               jnp.where(lane == 2, w1, jnp.where(lane == 3, w2, 0.0))))
    info_ref[...] = info


def _router(h, o, wo, g, rw):
    t, d = h.shape
    tm = TM_ROUTER
    row = lambda i: (i, 0)
    c2 = lambda i: (0, 0)
    return pl.pallas_call(
        _router_kernel,
        grid=(t // tm,),
        in_specs=[pl.BlockSpec((tm, d), row), pl.BlockSpec((tm, d), row), pl.BlockSpec((d, d), c2),
                  pl.BlockSpec((1, d), c2), pl.BlockSpec((d, LANES), c2)],
        out_specs=[pl.BlockSpec((tm, d), row), pl.BlockSpec((d // (2 * PIECE), tm, PIECE), lambda i: (0, i, 0)),
                   pl.BlockSpec((tm, LANES), row)],
        out_shape=[jax.ShapeDtypeStruct((t, d), F32), jax.ShapeDtypeStruct((d // (2 * PIECE), t, PIECE), jnp.uint32),
                   jax.ShapeDtypeStruct((t, LANES), F32)],
        compiler_params=_cparams("parallel"),
        name="router",
    )(h, o, wo, g, rw)


def _moe_kernel(te_ref, nu_ref, valid_ref, xs_ref, w1_ref, w3_ref, w2_ref, y_ref, h_sc):
    i = pl.program_id(0)

    @pl.when(i < nu_ref[0])
    def _():
        packed = _join_lanes(xs_ref)
        row = lax.broadcasted_iota(jnp.int32, packed.shape, 0)
        packed = jnp.where(row < valid_ref[i], packed, jnp.uint32(0))
        x = _unpack_bf16_pairs(packed).astype(BF16)
        _split_lanes(y_ref, _pack_bf16_pairs(_swiglu(x, w1_ref, w3_ref, w2_ref, h_sc, FC_MOE)))

    @pl.when(i >= nu_ref[0])
    def _():
        y_ref[...] = jnp.zeros_like(y_ref)


def _moe(tile_expert, n_used, valid, xs, w1, w3, w2):
    pieces, p, width = xs.shape
    tm = TM_MOE
    d, f = w1.shape[1:]
    grid_spec = pltpu.PrefetchScalarGridSpec(
        num_scalar_prefetch=3,
        grid=(p // tm,),
        in_specs=[pl.BlockSpec((pieces, tm, width), lambda i, te, nu, va: (0, i, 0)),
                  pl.BlockSpec((None, d, f), lambda i, te, nu, va: (te[i], 0, 0)),
                  pl.BlockSpec((None, d, f), lambda i, te, nu, va: (te[i], 0, 0)),
                  pl.BlockSpec((None, f, d), lambda i, te, nu, va: (te[i], 0, 0))],
        out_specs=pl.BlockSpec((pieces, tm, width), lambda i, te, nu, va: (0, i, 0)),
        scratch_shapes=[pltpu.VMEM((tm, f), BF16)],
    )
    return pl.pallas_call(
        _moe_kernel,
        grid_spec=grid_spec,
        out_shape=jax.ShapeDtypeStruct((pieces, p, width), jnp.uint32),
        compiler_params=pltpu.CompilerParams(dimension_semantics=("arbitrary",), vmem_limit_bytes=VMEM_LIMIT_MOE),
        name="moe",
    )(tile_expert, n_used, valid, xs, w1, w3, w2)


def _combine_kernel(h1_ref, y1_ref, y2_ref, info_ref, gf_ref, o_ref, *, final):
    info = info_ref[...]
    y1 = _unpack_bf16_pairs(_join_lanes(y1_ref))
    y2 = _unpack_bf16_pairs(_join_lanes(y2_ref))
    out = h1_ref[...] + info[:, 2:3] * y1 + info[:, 3:4] * y2
    if final:
        out = _rmsnorm(out, gf_ref[...])
    o_ref[...] = out


def _combine(h1, yg, info, gf, final):
    t, d = h1.shape
    tm = TM_COMBINE
    nt = t // tm
    return pl.pallas_call(
        functools.partial(_combine_kernel, final=final),
        grid=(nt,),
        in_specs=[pl.BlockSpec((tm, d), lambda i: (i, 0)),
                  pl.BlockSpec((d // (2 * PIECE), tm, PIECE), lambda i: (0, i, 0)),
                  pl.BlockSpec((d // (2 * PIECE), tm, PIECE), lambda i: (0, nt + i, 0)),
                  pl.BlockSpec((tm, LANES), lambda i: (i, 0)),
                  pl.BlockSpec((1, d), lambda i: (0, 0))],
        out_specs=pl.BlockSpec((tm, d), lambda i: (i, 0)),
        out_shape=jax.ShapeDtypeStruct((t, d), F32),
        compiler_params=_cparams("parallel"),
        name="combine",
    )(h1, yg, yg, info, gf)


def _sc_gather(x, idx):
    pieces, rows, lanes = x.shape
    idx = (jnp.arange(pieces, dtype=jnp.int32)[:, None] * rows + idx[None, :]).reshape(-1)
    out = _sc_gather_pieces(x.reshape(pieces * rows, lanes), idx)
    return out.reshape(pieces, -1, lanes)


def _sc_scatter(x, pos, out_rows):
    pieces, t, width = x.shape
    idx = (jnp.arange(pieces, dtype=jnp.int32)[:, None] * out_rows + pos[None, :]).reshape(-1)
    out = _sc_scatter_pieces(x.reshape(pieces * t, width), idx, pos.shape[0], pieces * out_rows)
    return out.reshape(pieces, out_rows, width)


def _sc_scatter_pieces(x, idx, slots, out_rows):
    n = idx.shape[0]
    d = x.shape[1]
    win = SC_GATHER_WINDOW
    pieces = n // slots
    nb_slots = slots // win
    nb_rows = x.shape[0] // pieces // win
    mesh = plsc.VectorSubcoreMesh(core_axis_name="c", subcore_axis_name="s")

    @functools.partial(pl.kernel, out_type=jax.ShapeDtypeStruct((out_rows, d), x.dtype), mesh=mesh)
    def scatter_kernel(x_hbm, i_hbm, o_hbm):
        def body(x_vmem, i_vmem):
            pltpu.sync_copy(x_vmem, o_hbm.at[i_vmem.at[0]])

        pltpu.emit_pipeline(
            body,
            grid=(n // win,),
            in_specs=[pl.BlockSpec((win, d), lambda i: ((i // nb_slots) * nb_rows + (i % nb_slots) % nb_rows, 0)),
                      pl.BlockSpec((1, win), lambda i: (0, i))],
            out_specs=[],
            core_axis_name=("c", "s"),
            dimension_semantics=(pltpu.PARALLEL,),
        )(x_hbm, i_hbm)

    return scatter_kernel(x, idx.reshape(1, n))


def _bf16_bits(a):
    bits = lax.bitcast_convert_type(a, jnp.uint32)
    return (bits + (jnp.uint32(0x7FFF) + ((bits >> 16) & jnp.uint32(1)))) >> 16


def _pack_bf16_pairs(x):
    w = x.shape[1] // 2
    return _bf16_bits(x[:, :w]) | (_bf16_bits(x[:, w:]) << 16)


def _unpack_bf16_pairs(p):
    lo = lax.bitcast_convert_type(p << 16, F32)
    hi = lax.bitcast_convert_type(p & jnp.uint32(0xFFFF0000), F32)
    return jnp.concatenate([lo, hi], axis=1)


def _split_lanes(ref, x):
    width = ref.shape[2]
    for k in range(ref.shape[0]):
        ref[k] = x[:, k * width:(k + 1) * width].astype(ref.dtype)


def _join_lanes(ref):
    return jnp.concatenate([ref[k] for k in range(ref.shape[0])], axis=1)


def _sc_gather_pieces(x, idx):
    n = idx.shape[0]
    d = x.shape[1]
    win = SC_GATHER_WINDOW
    mesh = plsc.VectorSubcoreMesh(core_axis_name="c", subcore_axis_name="s")

    @functools.partial(pl.kernel, out_type=jax.ShapeDtypeStruct((n, d), x.dtype), mesh=mesh)
    def gather_kernel(x_hbm, i_hbm, o_hbm):
        def body(i_vmem, o_vmem):
            pltpu.sync_copy(x_hbm.at[i_vmem.at[0]], o_vmem)

        pltpu.emit_pipeline(
            body,
            grid=(n // win,),
            in_specs=[pl.BlockSpec((1, win), lambda i: (0, i))],
            out_specs=[pl.BlockSpec((win, d), lambda i: (i, 0))],
            core_axis_name=("c", "s"),
            dimension_semantics=(pltpu.PARALLEL,),
        )(i_hbm, o_hbm)

    return gather_kernel(x, idx.reshape(1, n))


def _routing_plan(info, t):
    tm = TM_MOE
    n_tiles = (2 * t) // tm + N_EXPERTS
    top = jnp.concatenate([info[:, 0], info[:, 1]]).astype(jnp.int32)
    onehot = (top[:, None] == jnp.arange(N_EXPERTS, dtype=jnp.int32)[None, :]).astype(jnp.int32)
    csum = jnp.cumsum(onehot, axis=0)
    rank = jnp.sum((csum - onehot) * onehot, axis=1)
    counts = csum[-1]
    ntile = (counts + tm - 1) // tm
    tile_end = jnp.cumsum(ntile)
    tile_start = tile_end - ntile
    pos = tile_start[top] * tm + rank
    n_used = tile_end[-1]
    tile_ids = jnp.arange(n_tiles, dtype=jnp.int32)
    te = jnp.sum((tile_ids[:, None] >= tile_end[None, :]).astype(jnp.int32), axis=1)
    last = jnp.sum((n_used - 1 >= tile_end).astype(jnp.int32))
    te = jnp.where(tile_ids < n_used, te, last).astype(jnp.int32)
    valid = jnp.clip(counts[te] - (tile_ids - tile_start[te]) * tm, 0, tm)
    valid = jnp.where(tile_ids < n_used, valid, 0).astype(jnp.int32)
    return te, n_used.reshape(1).astype(jnp.int32), valid, pos, n_tiles * tm


def _pad_cols(w, n):
    return jnp.pad(w, ((0, 0), (0, n - w.shape[1])))


def _block_diag(w):
    nb, bs, _ = w.shape
    eye = jnp.eye(nb, dtype=w.dtype)
    return jnp.einsum('nij,nm->nimj', w, eye).reshape(nb * bs, nb * bs)


def kernel(x, norm_mix, norm_ffn, norm_final, ev_w_in, ev_conv_w, ev_conv_b, ev_ga_w, ev_ga_b, ev_gx_w, ev_gx_b, ev_lambda, ev_f_b, ev_w_out, ev_ffn_w1, ev_ffn_w3, ev_ffn_w2, od_w_in, od_gate_w2, od_gate_b, od_head_norm, od_w_out, od_router, od_exp_w1, od_exp_w3, od_exp_w2):
    bsz, seq, d = x.shape
    t = bsz * seq
    depth = norm_mix.shape[0]
    h = x.reshape(t, d)
    n_qkv = 3 * FOX_HEADS * FOX_HEAD_DIM
    n_gla_k = 2 * GLA_HEADS * GLA_DK
    n_gla_v = GLA_HEADS * GLA_DV
    n_even_main = 2 * LRU_WIDTH + n_qkv
    n_odd_main = n_gla_k + 2 * n_gla_v
    row = lambda v: v.reshape(1, -1)
    expert_w = None

    for layer in range(depth):
        j = layer // 2
        g_mix = row(norm_mix[layer])
        g_ffn = row(norm_ffn[layer])
        if layer % 2 == 0:
            w_in = ev_w_in[j]
            w = jnp.concatenate([w_in[:, :n_even_main], _pad_cols(w_in[:, n_even_main:], LANES)], axis=1).astype(BF16)
            (xy, qkv, f), ffn_w = _proj(h, g_mix, w, [(0, 2 * LRU_WIDTH, F32), (2 * LRU_WIDTH, n_qkv, BF16),
                                                      (n_even_main, LANES, F32)],
                                        cast=(ev_ffn_w1, ev_ffn_w3, ev_ffn_w2), layer=j)
            fb = _pad_cols(row(ev_f_b[j]), LANES)
            ya, c = _lru(xy, f, ev_conv_w[j], row(ev_conv_b[j]),
                         _block_diag(ev_ga_w[j]).astype(BF16), _block_diag(ev_gx_w[j]).astype(BF16),
                         row(ev_ga_b[j]), row(ev_gx_b[j]), row(ev_lambda[j]), fb, bsz, seq)
            nxt = min(j, od_exp_w1.shape[0] - 1)
            yb, expert_w = _attn(*_attn_prep(qkv, c, bsz, seq), (od_exp_w1, od_exp_w3, od_exp_w2), nxt, bsz, seq)
            h = _ffn(h, ya, yb, ev_w_out[j].astype(BF16), g_ffn, *ffn_w)
        else:
            w_in = od_w_in[j]
            w = jnp.concatenate([w_in[:, :n_odd_main], _pad_cols(w_in[:, n_odd_main:], LANES)], axis=1).astype(BF16)
            (qk, v, g, lr), _ = _proj(h, g_mix, w, [(0, n_gla_k, F32), (n_gla_k, n_gla_v, BF16),
                                                    (n_gla_k + n_gla_v, n_gla_v, F32), (n_odd_main, LANES, F32)])
            gw = jnp.pad(od_gate_w2[j], ((0, LANES - GLA_RANK), (0, 0))).astype(BF16)
            o = _gla(qk, v, g, lr, gw, row(od_gate_b[j]), row(od_head_norm[j]), bsz, seq)
            rw = _pad_cols(od_router[j], LANES)
            h1, xn, info = _router(h, o, od_w_out[j].astype(BF16), g_ffn, rw)
            te, n_used, valid, pos, p_rows = _routing_plan(info, t)
            xs = _sc_scatter(xn, pos, p_rows)
            y = _moe(te, n_used, valid, xs, *expert_w)
            yg = _sc_gather(y, pos)
            final = layer == depth - 1
            h = _combine(h1, yg, info, row(norm_final), final)
    if depth % 2 == 1:
        raise NotImplementedError("final norm is fused into the last odd layer")
    return h.reshape(bsz, seq, d)
```
